```python
import math
import jax, jax.numpy as jnp
from jax import lax
import numpy as np

D_MODEL = 1024
BATCH = 8
SEQ = 2048
DEPTH = 4
DEC_BATCH = 128
DEC_SEQ = 1
PAST_LEN = 16384
PAGE_SIZE = 128

D_MIX = D_MODEL
D_HGRN = D_MIX // 2
D_S5 = D_MIX - D_HGRN
HGRN_HEAD_DIM = 128
HGRN_HEADS = D_HGRN // HGRN_HEAD_DIM
HGRN_CHUNK = 64
S5_GROUP_CH = 16
S5_GROUPS = D_S5 // S5_GROUP_CH
S5_STATE = 64
D_FF = ((8 * D_MODEL // 3 + 127) // 128) * 128
CONV_W = 3
D_IN_PROJ = 4 * D_HGRN + D_S5
ALPHA = (2 * DEPTH) ** 0.25
BETA = (8 * DEPTH) ** -0.25
LN_EPS = 1e-5
RMS_EPS = 1e-6
F_FLOOR = 1e-20

kernel_name = "hymba_hgrn2_s5_convffn_deepnorm_step"


def layer_norm(x, g, b):
    xf = x.astype(jnp.float32)
    mu = jnp.mean(xf, axis=-1, keepdims=True)
    xc = xf - mu
    var = jnp.mean(xc * xc, axis=-1, keepdims=True)
    return (xc * lax.rsqrt(var + LN_EPS) * g.astype(jnp.float32) + b.astype(jnp.float32)).astype(x.dtype)


def hgrn2_chunked(q, log_f, k, v, s0):
    B, T, H, DK = q.shape
    DV = v.shape[-1]
    c = min(HGRN_CHUNK, T)
    n = -(-T // c)
    pad = n * c - T

    def to_chunks(a):
        a = jnp.pad(a, ((0, 0), (0, pad), (0, 0), (0, 0)))
        return a.reshape(B, n, c, H, a.shape[-1]).transpose(1, 0, 2, 3, 4)

    causal = jnp.tril(jnp.ones((c, c), dtype=bool))[None, :, :, None, None]

    def step(S, inp):
        qc, lfc, kc, vc = inp
        b = jnp.cumsum(lfc, axis=1)
        diff = b[:, :, None] - b[:, None]
        decay = jnp.where(causal, jnp.exp(jnp.where(causal, diff, 0.0)), 0.0)
        scores = jnp.einsum('bthd,btshd,bshd->btsh', qc, decay, kc)
        o = (jnp.einsum('btsh,bshe->bthe', scores, vc)
             + jnp.einsum('bthd,bhde->bthe', qc * jnp.exp(b), S))
        b_last = b[:, -1]
        S_new = (jnp.exp(b_last)[..., None] * S
                 + jnp.einsum('bshd,bshe->bhde', kc * jnp.exp(b_last[:, None] - b), vc))
        return S_new, o

    S, o = lax.scan(step, s0, (to_chunks(q), to_chunks(log_f), to_chunks(k), to_chunks(v)))
    o = o.transpose(1, 0, 2, 3, 4).reshape(B, n * c, H, DV)[:, :T]
    return o, S


def hgrn2_mixer(q, fz, iv, g, lb, norm_g, s0):
    B, T, _ = q.shape
    f32 = jnp.float32
    fz32 = fz.astype(f32)
    lb32 = lb.astype(f32)
    f = lb32 + (1.0 - lb32) * jax.nn.sigmoid(fz32)
    log_f = jnp.log(jnp.maximum(f, F_FLOOR))
    k = (1.0 - lb32) * jax.nn.sigmoid(-fz32)
    heads = lambda a: a.reshape(B, T, HGRN_HEADS, HGRN_HEAD_DIM)
    o, s_new = hgrn2_chunked(heads(q.astype(f32)), heads(log_f), heads(k),
                             heads(iv.astype(f32)), s0.astype(f32))
    o = o * lax.rsqrt(jnp.mean(o * o, axis=-1, keepdims=True) + RMS_EPS) * norm_g.astype(f32)
    o = o.reshape(B, T, D_HGRN) * jax.nn.sigmoid(g.astype(f32))
    return o, s_new


def _complex_affine_combine(e1, e2):
    a1r, a1i, b1r, b1i = e1
    a2r, a2i, b2r, b2i = e2
    return (a2r * a1r - a2i * a1i,
            a2r * a1i + a2i * a1r,
            a2r * b1r - a2i * b1i + b2r,
            a2r * b1i + a2i * b1r + b2i)


def s5_mixer(u, lam_re, lam_im, log_dt, b_re, b_im, c_re, c_im, d_skip, w_glu, h_re0, h_im0):
    B, T, _ = u.shape
    f32 = jnp.float32
    u32 = u.astype(f32)
    ug = u32.reshape(B, T, S5_GROUPS, S5_GROUP_CH)
    lr = jnp.minimum(lam_re.astype(f32), -1e-4)
    li = lam_im.astype(f32)
    dt = jnp.exp(log_dt.astype(f32))[:, None]
    mag = jnp.exp(lr * dt)
    ab_re = mag * jnp.cos(li * dt)
    ab_im = mag * jnp.sin(li * dt)
    den = lr * lr + li * li
    nr = ab_re - 1.0
    coef_re = (nr * lr + ab_im * li) / den
    coef_im = (ab_im * lr - nr * li) / den
    br = b_re.astype(f32)
    bi = b_im.astype(f32)
    bb_re = coef_re[..., None] * br - coef_im[..., None] * bi
    bb_im = coef_re[..., None] * bi + coef_im[..., None] * br
    bu_re = jnp.einsum('btgj,gpj->tbgp', ug, bb_re)
    bu_im = jnp.einsum('btgj,gpj->tbgp', ug, bb_im)
    hr = h_re0.astype(f32)
    hi = h_im0.astype(f32)
    bu_re = bu_re.at[0].add(ab_re * hr - ab_im * hi)
    bu_im = bu_im.at[0].add(ab_re * hi + ab_im * hr)
    a_re = jnp.broadcast_to(ab_re, (T, 1, S5_GROUPS, S5_STATE))
    a_im = jnp.broadcast_to(ab_im, (T, 1, S5_GROUPS, S5_STATE))
    _, _, xr, xi = lax.associative_scan(_complex_affine_combine, (a_re, a_im, bu_re, bu_im), axis=0)
    y = (jnp.einsum('tbgp,gjp->btgj', xr, c_re.astype(f32))
         - jnp.einsum('tbgp,gjp->btgj', xi, c_im.astype(f32)))
    y = y.reshape(B, T, D_S5) + d_skip.astype(f32) * u32
    y = jax.nn.gelu(y, approximate=False)
    y = y * jax.nn.sigmoid(y @ w_glu.astype(f32))
    return y, xr[-1], xi[-1]


def conv_ffn(x, w_up, conv_w, conv_b, w_down, buf):
    T = x.shape[1]
    up = x @ w_up
    padded = jnp.concatenate([buf.astype(up.dtype), up], axis=1)
    h = conv_b + padded[:, 0:T] * conv_w[0]
    for j in range(1, CONV_W):
        h = h + padded[:, j:j + T] * conv_w[j]
    val, gate = jnp.split(h, 2, axis=-1)
    out = (jax.nn.silu(gate) * val) @ w_down
    return out, padded[:, -(CONV_W - 1):]


def trunk(x, s_hgrn, s_re, s_im, buf, params):
    (w_in, lb_logits, hgrn_norm_g, lam_re, lam_im, log_dt, b_re, b_im, c_re, c_im, d_skip,
     w_glu, w_out, ln1_g, ln1_b, w_up, conv_w, conv_b, w_down, ln2_g, ln2_b) = params
    sm = jax.nn.softmax(lb_logits.astype(jnp.float32), axis=0)
    lower_bounds = jnp.cumsum(sm, axis=0) - sm[0:1]
    new_h, new_re, new_im, new_buf = [], [], [], []
    for l in range(DEPTH):
        proj = x @ w_in[l]
        q, fz, iv, g, u = jnp.split(proj, [D_HGRN, 2 * D_HGRN, 3 * D_HGRN, 4 * D_HGRN], axis=-1)
        o_a, sa = hgrn2_mixer(q, fz, iv, g, lower_bounds[l], hgrn_norm_g[l], s_hgrn[l])
        o_b, sr, si = s5_mixer(u, lam_re[l], lam_im[l], log_dt[l], b_re[l], b_im[l],
                               c_re[l], c_im[l], d_skip[l], w_glu[l], s_re[l], s_im[l])
        mix = jnp.concatenate([o_a.astype(x.dtype), o_b.astype(x.dtype)], axis=-1) @ w_out[l]
        x = layer_norm(ALPHA * x + mix, ln1_g[l], ln1_b[l])
        ff, nb = conv_ffn(x, w_up[l], conv_w[l], conv_b[l], w_down[l], buf[l])
        x = layer_norm(ALPHA * x + ff, ln2_g[l], ln2_b[l])
        new_h.append(sa)
        new_re.append(sr)
        new_im.append(si)
        new_buf.append(nb)
    return x, jnp.stack(new_h), jnp.stack(new_re), jnp.stack(new_im), jnp.stack(new_buf)


def setup_inputs(seed: int = 0) -> dict:
    key = jax.random.key(seed)
    ks = jax.random.split(key, 32)
    f32 = jnp.float32
    nrm = lambda k, shape, s: jax.random.normal(k, shape, f32) * s
    lam_im_base = math.pi * jnp.arange(S5_STATE, dtype=f32)
    return {
        "x_prompt": nrm(ks[0], (BATCH, SEQ, D_MODEL), 1.0),
        "x_sample": nrm(ks[1], (DEC_BATCH, DEC_SEQ, D_MODEL), 1.0),
        "state_hgrn": nrm(ks[2], (DEPTH, DEC_BATCH, HGRN_HEADS, HGRN_HEAD_DIM, HGRN_HEAD_DIM), 0.5),
        "state_s5_re": nrm(ks[3], (DEPTH, DEC_BATCH, S5_GROUPS, S5_STATE), 0.5),
        "state_s5_im": nrm(ks[4], (DEPTH, DEC_BATCH, S5_GROUPS, S5_STATE), 0.5),
        "cache_ffn_conv": nrm(ks[5], (DEPTH, DEC_BATCH, CONV_W - 1, 2 * D_FF), 1.0),
        "w_in": nrm(ks[6], (DEPTH, D_MODEL, D_IN_PROJ), D_MODEL ** -0.5),
        "hgrn_lb_logits": nrm(ks[7], (DEPTH, D_HGRN), 0.1),
        "hgrn_norm_g": 1.0 + nrm(ks[8], (DEPTH, HGRN_HEAD_DIM), 0.02),
        "s5_lambda_re": -0.5 + nrm(ks[9], (DEPTH, S5_GROUPS, S5_STATE), 0.01),
        "s5_lambda_im": lam_im_base + nrm(ks[10], (DEPTH, S5_GROUPS, S5_STATE), 0.01),
        "s5_log_dt": jax.random.uniform(ks[11], (DEPTH, S5_GROUPS), f32,
                                        math.log(0.001), math.log(0.1)),
        "s5_b_re": nrm(ks[12], (DEPTH, S5_GROUPS, S5_STATE, S5_GROUP_CH), (2 * S5_GROUP_CH) ** -0.5),
        "s5_b_im": nrm(ks[13], (DEPTH, S5_GROUPS, S5_STATE, S5_GROUP_CH), (2 * S5_GROUP_CH) ** -0.5),
        "s5_c_re": nrm(ks[14], (DEPTH, S5_GROUPS, S5_GROUP_CH, S5_STATE), (2 * S5_STATE) ** -0.5),
        "s5_c_im": nrm(ks[15], (DEPTH, S5_GROUPS, S5_GROUP_CH, S5_STATE), (2 * S5_STATE) ** -0.5),
        "s5_d": nrm(ks[16], (DEPTH, D_S5), 1.0),
        "w_glu": nrm(ks[17], (DEPTH, D_S5, D_S5), D_S5 ** -0.5),
        "w_out": nrm(ks[18], (DEPTH, D_MIX, D_MODEL), BETA * D_MIX ** -0.5),
        "ln1_g": 1.0 + nrm(ks[19], (DEPTH, D_MODEL), 0.02),
        "ln1_b": nrm(ks[20], (DEPTH, D_MODEL), 0.01),
        "w_ffn_up": nrm(ks[21], (DEPTH, D_MODEL, 2 * D_FF), D_MODEL ** -0.5),
        "ffn_conv_w": nrm(ks[22], (DEPTH, CONV_W, 2 * D_FF), CONV_W ** -0.5),
        "ffn_conv_b": nrm(ks[23], (DEPTH, 2 * D_FF), 0.01),
        "w_ffn_down": nrm(ks[24], (DEPTH, D_FF, D_MODEL), BETA * D_FF ** -0.5),
        "ln2_g": 1.0 + nrm(ks[25], (DEPTH, D_MODEL), 0.02),
        "ln2_b": nrm(ks[26], (DEPTH, D_MODEL), 0.01),
    }


def reference(x_prompt, x_sample, state_hgrn, state_s5_re, state_s5_im, cache_ffn_conv,
              w_in, hgrn_lb_logits, hgrn_norm_g, s5_lambda_re, s5_lambda_im, s5_log_dt,
              s5_b_re, s5_b_im, s5_c_re, s5_c_im, s5_d, w_glu, w_out, ln1_g, ln1_b,
              w_ffn_up, ffn_conv_w, ffn_conv_b, w_ffn_down, ln2_g, ln2_b):
    params = (w_in, hgrn_lb_logits, hgrn_norm_g, s5_lambda_re, s5_lambda_im, s5_log_dt,
              s5_b_re, s5_b_im, s5_c_re, s5_c_im, s5_d, w_glu, w_out, ln1_g, ln1_b,
              w_ffn_up, ffn_conv_w, ffn_conv_b, w_ffn_down, ln2_g, ln2_b)
    bp = x_prompt.shape[0]
    f32 = jnp.float32
    zero_h = jnp.zeros((DEPTH, bp, HGRN_HEADS, HGRN_HEAD_DIM, HGRN_HEAD_DIM), f32)
    zero_s = jnp.zeros((DEPTH, bp, S5_GROUPS, S5_STATE), f32)
    zero_buf = jnp.zeros((DEPTH, bp, CONV_W - 1, 2 * D_FF), x_prompt.dtype)
    y_prompt, h_p, re_p, im_p, buf_p = trunk(x_prompt, zero_h, zero_s, zero_s, zero_buf, params)
    y_sample, h_s, re_s, im_s, buf_s = trunk(x_sample, state_hgrn, state_s5_re, state_s5_im,
                                             cache_ffn_conv, params)
    return (y_prompt, y_sample, h_p, re_p, im_p, buf_p, h_s, re_s, im_s, buf_s)
```

```python
import functools
import math

import jax
import jax.numpy as jnp
import numpy as np
from jax import lax
from jax.experimental import pallas as pl
from jax.experimental.pallas import tpu as pltpu

F32 = jnp.float32
BF16 = jnp.bfloat16

HEAD_DIM = 128
CHUNK = 64
S5_GROUP_CH = 16
S5_STATE = 64
LANES = 128
SUBLANES = 8
LN_EPS = 1e-5
RMS_EPS = 1e-6
F_FLOOR = 1e-20
SAFE_SPAN = 60.0
VMEM_LIMIT = 56 * 1024 * 1024

HGRN_TILE = 256
S5_TILE = 32
S5_PAD = 4
FFN_TILE = 256
FFN_COLS = 256
SAMPLE_ROWS = 8


def _dot(a, b):
    return jnp.dot(a, b, preferred_element_type=F32)


def _dot_nt(a, b):
    return lax.dot_general(a, b, (((1,), (1,)), ((), ())), preferred_element_type=F32)


def _dot_tn(a, b):
    return lax.dot_general(a, b, (((0,), (0,)), ((), ())), preferred_element_type=F32)


def _split3(x):
    hi = x.astype(BF16)
    r1 = x - hi.astype(F32)
    mid = r1.astype(BF16)
    lo = (r1 - mid.astype(F32)).astype(BF16)
    return hi, mid, lo


def _dot3(m, parts):
    return _dot(m, parts[0]) + _dot(m, parts[1]) + _dot(m, parts[2])


def _layer_norm(x, g, b):
    mu = jnp.mean(x, axis=-1, keepdims=True)
    xc = x - mu
    var = jnp.mean(xc * xc, axis=-1, keepdims=True)
    return xc * lax.rsqrt(var + LN_EPS) * g + b


def _hgrn_gates(fz, lb):
    oml = 1.0 - lb
    f = lb + oml * jax.nn.sigmoid(fz)
    log_f = jnp.log(jnp.maximum(f, F_FLOOR))
    k = oml * jax.nn.sigmoid(-fz)
    return log_f, k


def _head_rms_gate(o, g, ng):
    heads = o.shape[-1] // HEAD_DIM
    outs = []
    for h in range(heads):
        oh = o[:, h * HEAD_DIM:(h + 1) * HEAD_DIM]
        ms = jnp.mean(oh * oh, axis=-1, keepdims=True)
        outs.append(oh * lax.rsqrt(ms + RMS_EPS))
    return jnp.concatenate(outs, axis=-1) * ng * jax.nn.sigmoid(g)


def _hgrn_prompt_kernel(x_ref, w_ref, lb_ref, ng_ref, tri_ref, lvl_ref, oa_ref, s_ref,
                        st_ref, sc_ref, *, tile, heads):
    t = pl.program_id(1)
    d_h = heads * HEAD_DIM

    @pl.when(t == 0)
    def _():
        st_ref[...] = jnp.zeros_like(st_ref)

    proj = _dot(x_ref[...].astype(BF16), w_ref[...])
    lb = lb_ref[...]
    ng = ng_ref[...]
    ri = lax.broadcasted_iota(jnp.int32, (CHUNK, CHUNK), 0)
    ci = lax.broadcasted_iota(jnp.int32, (CHUNK, CHUNK), 1)
    causal = ri >= ci
    row = lax.broadcasted_iota(jnp.int32, (CHUNK, 1), 0)

    for c in range(tile // CHUNK):
        r0 = c * CHUNK
        q = proj[r0:r0 + CHUNK, 0:d_h]
        fz = proj[r0:r0 + CHUNK, d_h:2 * d_h]
        v = proj[r0:r0 + CHUNK, 2 * d_h:3 * d_h]
        g = proj[r0:r0 + CHUNK, 3 * d_h:4 * d_h]
        log_f, k = _hgrn_gates(fz, lb)
        parts = _split3(log_f)
        b = _dot3(tri_ref[...], parts)
        mid = CHUNK // 2 - 1
        bm = b[mid:mid + 1, :]
        bl = b[CHUNK - 1:CHUNK, :]
        span_ok = jnp.min(jnp.minimum(bm, bl - bm)) >= -SAFE_SPAN

        @pl.when(span_ok)
        def _():
            qe = (q * jnp.exp(b - bm)).astype(BF16)
            ke = (k * jnp.exp(bm - b)).astype(BF16)
            for h in range(heads):
                sl = slice(h * HEAD_DIM, (h + 1) * HEAD_DIM)
                sc_ref[h] = jnp.where(causal, _dot_nt(qe[:, sl], ke[:, sl]), 0.0)

        @pl.when(jnp.logical_not(span_ok))
        def _():
            anchors = _dot3(lvl_ref[...], parts)
            qb = q.astype(BF16)
            kb = k.astype(BF16)
            for h in range(heads):
                sl = slice(h * HEAD_DIM, (h + 1) * HEAD_DIM)
                sc_ref[h] = jnp.where(ri == ci, _dot_nt(qb[:, sl], kb[:, sl]), 0.0)
            lvl = 0
            hs = CHUNK // 2
            while hs >= 1:
                a = anchors[lvl * CHUNK:(lvl + 1) * CHUNK, :]
                upper = (row & (2 * hs - 1)) >= hs
                eq = jnp.where(upper, jnp.exp(jnp.minimum(b - a, 0.0)), 0.0)
                ek = jnp.where(upper, 0.0, jnp.exp(jnp.minimum(a - b, 0.0)))
                qe = (q * eq).astype(BF16)
                ke = (k * ek).astype(BF16)
                shift = int(math.log2(2 * hs))
                same = (ri >> shift) == (ci >> shift)
                for h in range(heads):
                    sl = slice(h * HEAD_DIM, (h + 1) * HEAD_DIM)
                    sc_ref[h] += jnp.where(same, _dot_nt(qe[:, sl], ke[:, sl]), 0.0)
                lvl += 1
                hs //= 2

        qs = (q * jnp.exp(b)).astype(BF16)
        kl = (k * jnp.exp(bl - b)).astype(BF16)
        vb = v.astype(BF16)
        dec = jnp.exp(bl)
        outs = []
        for h in range(heads):
            sl = slice(h * HEAD_DIM, (h + 1) * HEAD_DIM)
            st = st_ref[h]
            o = _dot(sc_ref[h].astype(BF16), vb[:, sl]) + _dot_nt(qs[:, sl], st.astype(BF16))
            st_ref[h] = st * dec[:, sl] + _dot_tn(vb[:, sl], kl[:, sl])
            outs.append(o)
        o = jnp.concatenate(outs, axis=-1)
        oa_ref[r0:r0 + CHUNK, :] = _head_rms_gate(o, g, ng).astype(oa_ref.dtype)

    @pl.when(t == pl.num_programs(1) - 1)
    def _():
        for h in range(heads):
            s_ref[h] = st_ref[h].T


def _chunk_matrices():
    r = np.arange(CHUNK)
    tri = (r[None, :] <= r[:, None]).astype(np.float32)
    lvls = []
    hs = CHUNK // 2
    while hs >= 1:
        anchor = (r // (2 * hs)) * (2 * hs) + hs - 1
        lvls.append((r[None, :] <= anchor[:, None]).astype(np.float32))
        hs //= 2
    return jnp.asarray(tri, BF16), jnp.asarray(np.concatenate(lvls, axis=0), BF16)


def _hgrn_prompt(x, w_h, lb, ng):
    bsz, seq, d_model = x.shape
    d_h = lb.shape[-1]
    heads = d_h // HEAD_DIM
    tile = min(HGRN_TILE, seq)
    assert seq % tile == 0 and tile % CHUNK == 0
    tri, lvl = _chunk_matrices()
    const = lambda *shape: pl.BlockSpec(shape, lambda b, t: (0,) * len(shape))
    return pl.pallas_call(
        functools.partial(_hgrn_prompt_kernel, tile=tile, heads=heads),
        grid=(bsz, seq // tile),
        in_specs=[
            pl.BlockSpec((None, tile, d_model), lambda b, t: (b, t, 0)),
            const(d_model, 4 * d_h),
            const(1, d_h),
            const(1, d_h),
            const(CHUNK, CHUNK),
            const(lvl.shape[0], CHUNK),
        ],
        out_specs=[
            pl.BlockSpec((None, tile, d_h), lambda b, t: (b, t, 0)),
            pl.BlockSpec((None, heads, HEAD_DIM, HEAD_DIM), lambda b, t: (b, 0, 0, 0)),
        ],
        out_shape=[
            jax.ShapeDtypeStruct((bsz, seq, d_h), BF16),
            jax.ShapeDtypeStruct((bsz, heads, HEAD_DIM, HEAD_DIM), F32),
        ],
        scratch_shapes=[
            pltpu.VMEM((heads, HEAD_DIM, HEAD_DIM), F32),
            pltpu.VMEM((heads, CHUNK, CHUNK), F32),
        ],
        compiler_params=pltpu.CompilerParams(
            dimension_semantics=("arbitrary", "arbitrary"), vmem_limit_bytes=VMEM_LIMIT),
        name="hgrn_prompt",
    )(x, w_h, lb, ng, tri, lvl)


def _s5_input_drive(ub, bre_ref, bim_ref, store):
    kblocks = bre_ref.shape[0]
    slabs_per_block = bre_ref.shape[2] // LANES
    for m in range(kblocks):
        um = ub[:, m * LANES:(m + 1) * LANES]
        re = _dot(um, bre_ref[m])
        im = _dot(um, bim_ref[m])
        for s in range(slabs_per_block):
            store(m * slabs_per_block + s, re[:, s * LANES:(s + 1) * LANES], im[:, s * LANES:(s + 1) * LANES])


def _s5_readout(load, c_ref, d_s5):
    slabs = c_ref.shape[0]
    tiles = d_s5 // (2 * LANES)
    per_tile = slabs // tiles
    ys = []
    for n in range(tiles):
        acc = None
        for s in range(n * per_tile, (n + 1) * per_tile):
            part = _dot(load(s), c_ref[s])
            acc = part if acc is None else acc + part
        ys.append(acc)
    return jnp.concatenate(ys, axis=-1)


def _s5_output(y, u, d_ref, wglu_ref):
    y = y + d_ref[...] * u
    y = 0.5 * y * (1.0 + lax.erf(y * math.sqrt(0.5)))
    return y * jax.nn.sigmoid(_dot(y.astype(BF16), wglu_ref[...]))


def _s5_prompt_kernel(x_ref, wu_ref, bre_ref, bim_ref, c_ref, are_ref, aim_ref, d_ref, wglu_ref,
                      ob_ref, hre_ref, him_ref, xr_ref, xi_ref, y_ref, *, tile, bsz):
    i = pl.program_id(0)
    pitch = tile + S5_PAD
    slabs = xr_ref.shape[0]
    d_s5 = d_ref.shape[-1]

    @pl.when(i == 0)
    def _():
        hre_ref[...] = jnp.zeros_like(hre_ref)
        him_ref[...] = jnp.zeros_like(him_ref)
        xr_ref[...] = jnp.zeros_like(xr_ref)
        xi_ref[...] = jnp.zeros_like(xi_ref)

    x = x_ref[...].reshape(bsz * tile, x_ref.shape[-1])
    u = _dot(x.astype(BF16), wu_ref[...])

    def store(slab, re, im):
        for b in range(bsz):
            xr_ref[slab, b * pitch:b * pitch + tile, :] = re[b * tile:(b + 1) * tile, :]
            xi_ref[slab, b * pitch:b * pitch + tile, :] = im[b * tile:(b + 1) * tile, :]

    _s5_input_drive(u.astype(BF16), bre_ref, bim_ref, store)

    def step(tt, carry):
        hr, hi = carry
        nhr, nhi = [], []
        for s in range(slabs):
            ar = are_ref[:, s * LANES:(s + 1) * LANES]
            ai = aim_ref[:, s * LANES:(s + 1) * LANES]
            rows = pl.ds(tt, bsz, stride=pitch)
            nr = ar * hr[s] - ai * hi[s] + xr_ref[s, rows, :]
            ni = ar * hi[s] + ai * hr[s] + xi_ref[s, rows, :]
            xr_ref[s, rows, :] = nr
            xi_ref[s, rows, :] = ni
            nhr.append(nr)
            nhi.append(ni)
        return tuple(nhr), tuple(nhi)

    hr0 = tuple(hre_ref[:, s * LANES:(s + 1) * LANES] for s in range(slabs))
    hi0 = tuple(him_ref[:, s * LANES:(s + 1) * LANES] for s in range(slabs))
    hr, hi = lax.fori_loop(0, tile, step, (hr0, hi0))
    for s in range(slabs):
        hre_ref[:, s * LANES:(s + 1) * LANES] = hr[s]
        him_ref[:, s * LANES:(s + 1) * LANES] = hi[s]

    load = lambda s: jnp.concatenate([xr_ref[s], xi_ref[s]], axis=-1).astype(BF16)
    y_ref[...] = _s5_readout(load, c_ref, d_s5)
    y = jnp.concatenate([y_ref[b * pitch:b * pitch + tile, :] for b in range(bsz)], axis=0)
    o = _s5_output(y, u, d_ref, wglu_ref)
    ob_ref[...] = o.reshape(bsz, tile, d_s5).astype(ob_ref.dtype)


def _s5_prompt(x, w_u, s5p):
    bsz, seq, d_model = x.shape
    bre, bim, cblk, are, aim, d_skip, wglu = s5p
    d_s5 = d_skip.shape[-1]
    n_state = are.shape[-1]
    slabs = n_state // LANES
    tile = min(S5_TILE, seq)
    assert seq % tile == 0 and bsz == SUBLANES
    pitch = tile + S5_PAD
    const = lambda *shape: pl.BlockSpec(shape, lambda i: (0,) * len(shape))
    return pl.pallas_call(
        functools.partial(_s5_prompt_kernel, tile=tile, bsz=bsz),
        grid=(seq // tile,),
        in_specs=[
            pl.BlockSpec((bsz, tile, d_model), lambda i: (0, i, 0)),
            const(*w_u.shape), const(*bre.shape), const(*bim.shape), const(*cblk.shape),
            const(1, n_state), const(1, n_state), const(1, d_s5), const(*wglu.shape),
        ],
        out_specs=[
            pl.BlockSpec((bsz, tile, d_s5), lambda i: (0, i, 0)),
            const(bsz, n_state), const(bsz, n_state),
        ],
        out_shape=[
            jax.ShapeDtypeStruct((bsz, seq, d_s5), BF16),
            jax.ShapeDtypeStruct((bsz, n_state), F32),
            jax.ShapeDtypeStruct((bsz, n_state), F32),
        ],
        scratch_shapes=[
            pltpu.VMEM((slabs, bsz * pitch, LANES), F32),
            pltpu.VMEM((slabs, bsz * pitch, LANES), F32),
            pltpu.VMEM((bsz * pitch, d_s5), F32),
        ],
        compiler_params=pltpu.CompilerParams(
            dimension_semantics=("arbitrary",), vmem_limit_bytes=VMEM_LIMIT),
        name="s5_prompt",
    )(x, w_u, bre, bim, cblk, are, aim, d_skip, wglu)


def _mix_sample_kernel(x_ref, w_ref, lb_ref, ng_ref, s_ref, hre_ref, him_ref,
                       bre_ref, bim_ref, c_ref, are_ref, aim_ref, d_ref, wglu_ref,
                       oa_ref, ob_ref, so_ref, nre_ref, nim_ref,
                       proj_ref, orow_ref, *, heads):
    i = pl.program_id(0)
    d_h = heads * HEAD_DIM
    d_s5 = d_ref.shape[-1]
    slabs = are_ref.shape[-1] // LANES

    @pl.when(i == 0)
    def _():
        proj = _dot(x_ref[...].astype(BF16), w_ref[...])
        proj_ref[...] = proj
        u = proj[:, 4 * d_h:]

        def store(slab, re, im):
            sl = slice(slab * LANES, (slab + 1) * LANES)
            ar, ai = are_ref[:, sl], aim_ref[:, sl]
            hr, hi = hre_ref[:, sl], him_ref[:, sl]
            nre_ref[:, sl] = ar * hr - ai * hi + re
            nim_ref[:, sl] = ar * hi + ai * hr + im

        _s5_input_drive(u.astype(BF16), bre_ref, bim_ref, store)
        load = lambda s: jnp.concatenate(
            [nre_ref[:, s * LANES:(s + 1) * LANES], nim_ref[:, s * LANES:(s + 1) * LANES]],
            axis=-1).astype(BF16)
        y = _s5_readout(load, c_ref, d_s5)
        ob_ref[...] = _s5_output(y, u, d_ref, wglu_ref).astype(ob_ref.dtype)

    r0 = pl.multiple_of(i * SAMPLE_ROWS, SAMPLE_ROWS)
    rows = proj_ref[pl.ds(r0, SAMPLE_ROWS), :]
    q = rows[:, 0:d_h]
    fz = rows[:, d_h:2 * d_h]
    v = rows[:, 2 * d_h:3 * d_h]
    g = rows[:, 3 * d_h:4 * d_h]
    log_f, k = _hgrn_gates(fz, lb_ref[...])
    f = jnp.exp(log_f)
    per = heads * SAMPLE_ROWS
    pieces = [a[:, h * HEAD_DIM:(h + 1) * HEAD_DIM] for a in (q, f, k) for h in range(heads)]
    assert 3 * per <= HEAD_DIM
    pieces.append(jnp.zeros((HEAD_DIM - 3 * per, HEAD_DIM), F32))
    cols = jnp.concatenate(pieces, axis=0).T
    for h in range(heads):
        for r in range(SAMPLE_ROWS):
            j = h * SAMPLE_ROWS + r
            qc = cols[:, j:j + 1]
            fc = cols[:, per + j:per + j + 1]
            kc = cols[:, 2 * per + j:2 * per + j + 1]
            sn = fc * s_ref[r, h] + kc * v[r:r + 1, h * HEAD_DIM:(h + 1) * HEAD_DIM]
            so_ref[r, h] = sn
            orow_ref[r:r + 1, h * HEAD_DIM:(h + 1) * HEAD_DIM] = jnp.sum(qc * sn, axis=0, keepdims=True)
    oa_ref[pl.ds(r0, SAMPLE_ROWS), :] = _head_rms_gate(orow_ref[...], g, ng_ref[...]).astype(oa_ref.dtype)


def _mix_sample(x, w_in, lb, ng, s_h, h_re, h_im, s5p):
    n, d_model = x.shape
    bre, bim, cblk, are, aim, d_skip, wglu = s5p
    d_h = lb.shape[-1]
    heads = d_h // HEAD_DIM
    d_s5 = d_skip.shape[-1]
    n_state = are.shape[-1]
    assert n % SAMPLE_ROWS == 0
    const = lambda *shape: pl.BlockSpec(shape, lambda i: (0,) * len(shape))
    state_spec = pl.BlockSpec((SAMPLE_ROWS, heads, HEAD_DIM, HEAD_DIM), lambda i: (i, 0, 0, 0))
    return pl.pallas_call(
        functools.partial(_mix_sample_kernel, heads=heads),
        grid=(n // SAMPLE_ROWS,),
        in_specs=[
            const(n, d_model), const(*w_in.shape), const(1, d_h), const(1, d_h),
            state_spec, const(n, n_state), const(n, n_state),
            const(*bre.shape), const(*bim.shape), const(*cblk.shape),
            const(1, n_state), const(1, n_state), const(1, d_s5), const(*wglu.shape),
        ],
        out_specs=[
            const(n, d_h), const(n, d_s5), state_spec, const(n, n_state), const(n, n_state),
        ],
        out_shape=[
            jax.ShapeDtypeStruct((n, d_h), BF16),
            jax.ShapeDtypeStruct((n, d_s5), BF16),
            jax.ShapeDtypeStruct(s_h.shape, F32),
            jax.ShapeDtypeStruct((n, n_state), F32),
            jax.ShapeDtypeStruct((n, n_state), F32),
        ],
        scratch_shapes=[
            pltpu.VMEM((n, w_in.shape[1]), F32),
            pltpu.VMEM((SAMPLE_ROWS, d_h), F32),
        ],
        compiler_params=pltpu.CompilerParams(
            dimension_semantics=("arbitrary",), vmem_limit_bytes=VMEM_LIMIT),
        name="mix_sample",
    )(x, w_in, lb, ng, s_h, h_re, h_im, bre, bim, cblk, are, aim, d_skip, wglu)


def _mix_ln1(x, oa, ob, wout_ref, g1_ref, b1_ref, alpha):
    d_a = oa.shape[-1]
    mix = _dot(oa, wout_ref[0:d_a, :]) + _dot(ob, wout_ref[d_a:, :])
    return _layer_norm(alpha * x + mix, g1_ref[...], b1_ref[...])


def _ffn_blocks(x1b, wup_ref, cw_ref, cb_ref, wdown_ref, taps, d_ff):
    acc = None
    for j in range(d_ff // FFN_COLS):
        hs = []
        for base in (0, d_ff):
            cols = slice(base + j * FFN_COLS, base + (j + 1) * FFN_COLS)
            up = _dot(x1b, wup_ref[:, cols])
            m2, m1 = taps(up, cols)
            hs.append(cb_ref[:, cols] + m2 * cw_ref[0:1, cols] + m1 * cw_ref[1:2, cols]
                      + up * cw_ref[2:3, cols])
        val, gate = hs
        hh = (gate * jax.nn.sigmoid(gate)) * val
        part = _dot(hh.astype(BF16), wdown_ref[j * FFN_COLS:(j + 1) * FFN_COLS, :])
        acc = part if acc is None else acc + part
    return acc


def _ffn_prompt_kernel(x_ref, oa_ref, ob_ref, wout_ref, g1_ref, b1_ref, wup_ref, cw_ref, cb_ref,
                       wdown_ref, g2_ref, b2_ref, y_ref, cache_ref, up_ref, *, tile, alpha, d_ff):
    t = pl.program_id(1)
    lead = SUBLANES

    @pl.when(t == 0)
    def _():
        up_ref[0:lead, :] = jnp.zeros((lead, up_ref.shape[1]), F32)

    x1 = _mix_ln1(x_ref[...], oa_ref[...], ob_ref[...], wout_ref, g1_ref, b1_ref, alpha)

    def taps(up, cols):
        up_ref[lead:lead + tile, cols] = up
        return up_ref[lead - 2:lead - 2 + tile, cols], up_ref[lead - 1:lead - 1 + tile, cols]

    ff = _ffn_blocks(x1.astype(BF16), wup_ref, cw_ref, cb_ref, wdown_ref, taps, d_ff)
    y_ref[...] = _layer_norm(alpha * x1 + ff, g2_ref[...], b2_ref[...])
    last = up_ref[lead + tile - 2:lead + tile, :]
    up_ref[lead - 2:lead, :] = last
    cache_ref[...] = last


def _ffn_sample_kernel(x_ref, oa_ref, ob_ref, cache_ref, wout_ref, g1_ref, b1_ref, wup_ref, cw_ref,
                       cb_ref, wdown_ref, g2_ref, b2_ref, y_ref, ncache_ref, *, alpha, d_ff):
    width = 2 * d_ff
    x1 = _mix_ln1(x_ref[...], oa_ref[...], ob_ref[...], wout_ref, g1_ref, b1_ref, alpha)

    def taps(up, cols):
        m2 = cache_ref[:, cols]
        m1 = cache_ref[:, slice(width + cols.start, width + cols.stop)]
        ncache_ref[:, cols] = m1
        ncache_ref[:, slice(width + cols.start, width + cols.stop)] = up
        return m2, m1

    ff = _ffn_blocks(x1.astype(BF16), wup_ref, cw_ref, cb_ref, wdown_ref, taps, d_ff)
    y_ref[...] = _layer_norm(alpha * x1 + ff, g2_ref[...], b2_ref[...])


def _resident(shape, grid_rank):
    zeros = (0,) * len(shape)
    index_map = {1: lambda i: zeros, 2: lambda b, t: zeros}[grid_rank]
    return pl.BlockSpec(shape, index_map, pipeline_mode=pl.Buffered(1))


def _ffn_prompt(x, oa, ob, fw, alpha):
    bsz, seq, d_model = x.shape
    wout, g1, b1, wup, cw, cb, wdown, g2, b2 = fw
    d_ff = wdown.shape[0]
    tile = min(FFN_TILE, seq)
    assert seq % tile == 0 and d_ff % FFN_COLS == 0
    res = lambda a: _resident(a.shape, 2)
    tok = lambda d: pl.BlockSpec((None, tile, d), lambda b, t: (b, t, 0))
    return pl.pallas_call(
        functools.partial(_ffn_prompt_kernel, tile=tile, alpha=alpha, d_ff=d_ff),
        grid=(bsz, seq // tile),
        in_specs=[tok(d_model), tok(oa.shape[-1]), tok(ob.shape[-1]),
                  res(wout), res(g1), res(b1), res(wup), res(cw), res(cb), res(wdown), res(g2), res(b2)],
        out_specs=[tok(d_model), pl.BlockSpec((None, 2, 2 * d_ff), lambda b, t: (b, 0, 0))],
        out_shape=[jax.ShapeDtypeStruct(x.shape, F32),
                   jax.ShapeDtypeStruct((bsz, 2, 2 * d_ff), F32)],
        scratch_shapes=[pltpu.VMEM((SUBLANES + tile, 2 * d_ff), F32)],
        compiler_params=pltpu.CompilerParams(
            dimension_semantics=("arbitrary", "arbitrary"), vmem_limit_bytes=VMEM_LIMIT),
        name="ffn_prompt",
    )(x, oa, ob, wout, g1, b1, wup, cw, cb, wdown, g2, b2)


def _ffn_sample(x, oa, ob, cache, fw, alpha):
    n, d_model = x.shape
    wout, g1, b1, wup, cw, cb, wdown, g2, b2 = fw
    d_ff = wdown.shape[0]
    args = (x, oa, ob, cache, wout, g1, b1, wup, cw, cb, wdown, g2, b2)
    return pl.pallas_call(
        functools.partial(_ffn_sample_kernel, alpha=alpha, d_ff=d_ff),
        grid=(1,),
        in_specs=[_resident(a.shape, 1) for a in args],
        out_specs=[pl.BlockSpec(x.shape, lambda i: (0, 0)), pl.BlockSpec(cache.shape, lambda i: (0, 0))],
        out_shape=[jax.ShapeDtypeStruct(x.shape, F32), jax.ShapeDtypeStruct(cache.shape, F32)],
        compiler_params=pltpu.CompilerParams(
            dimension_semantics=("arbitrary",), vmem_limit_bytes=VMEM_LIMIT),
        name="ffn_sample",
    )(*args)


def _s5_params(lam_re, lam_im, log_dt, b_re, b_im, c_re, c_im, d_skip, w_glu):
    groups, n_p = lam_re.shape
    lr = jnp.minimum(lam_re, -1e-4)
    li = lam_im
    dt = jnp.exp(log_dt)[:, None]
    mag = jnp.exp(lr * dt)
    ab_re = mag * jnp.cos(li * dt)
    ab_im = mag * jnp.sin(li * dt)
    den = lr * lr + li * li
    nr = ab_re - 1.0
    coef_re = (nr * lr + ab_im * li) / den
    coef_im = (ab_im * lr - nr * li) / den
    bb_re = coef_re[..., None] * b_re - coef_im[..., None] * b_im
    bb_im = coef_re[..., None] * b_im + coef_im[..., None] * b_re

    gpb = LANES // S5_GROUP_CH
    kblocks = groups // gpb
    eye = jnp.eye(gpb, dtype=F32)

    def b_blocks(bb):
        bb = bb.reshape(kblocks, gpb, n_p, S5_GROUP_CH)
        return jnp.einsum('mgpj,gh->mgjhp', bb, eye).reshape(kblocks, LANES, gpb * n_p).astype(BF16)

    gps = LANES // n_p
    slabs = groups // gps
    gpt = (2 * LANES) // S5_GROUP_CH
    pos = (jnp.arange(slabs)[:, None] * gps + jnp.arange(gps)[None, :]) % gpt
    onehot = jax.nn.one_hot(pos, gpt, dtype=F32)

    def c_half(c):
        c = c.reshape(slabs, gps, S5_GROUP_CH, n_p)
        return jnp.einsum('sgjp,sgh->sgphj', c, onehot).reshape(slabs, gps * n_p, gpt * S5_GROUP_CH)

    cblk = jnp.concatenate([c_half(c_re), -c_half(c_im)], axis=1).astype(BF16)
    return (b_blocks(bb_re), b_blocks(bb_im), cblk,
            ab_re.reshape(1, groups * n_p), ab_im.reshape(1, groups * n_p),
            d_skip.reshape(1, -1), w_glu.astype(BF16))


def kernel(x_prompt, x_sample, state_hgrn, state_s5_re, state_s5_im, cache_ffn_conv, w_in, hgrn_lb_logits, hgrn_norm_g, s5_lambda_re, s5_lambda_im, s5_log_dt, s5_b_re, s5_b_im, s5_c_re, s5_c_im, s5_d, w_glu, w_out, ln1_g, ln1_b, w_ffn_up, ffn_conv_w, ffn_conv_b, w_ffn_down, ln2_g, ln2_b):
    depth = w_in.shape[0]
    d_h = hgrn_lb_logits.shape[-1]
    heads = d_h // HEAD_DIM
    n_dec = x_sample.shape[0]
    groups, n_p = s5_lambda_re.shape[1:]
    alpha = (2 * depth) ** 0.25

    sm = jax.nn.softmax(hgrn_lb_logits.astype(F32), axis=0)
    lower_bounds = jnp.cumsum(sm, axis=0) - sm[0:1]

    xp = x_prompt
    xs = x_sample.reshape(n_dec, -1)
    outs = {k: [] for k in ("hp", "rp", "ip", "cp", "hs", "rs", "is", "cs")}
    for l in range(depth):
        w_l = w_in[l].astype(BF16)
        lb = lower_bounds[l].reshape(1, d_h)
        ng = jnp.tile(hgrn_norm_g[l], heads).reshape(1, d_h)
        s5p = _s5_params(s5_lambda_re[l], s5_lambda_im[l], s5_log_dt[l], s5_b_re[l], s5_b_im[l],
                         s5_c_re[l], s5_c_im[l], s5_d[l], w_glu[l])
        row = lambda a: a.reshape(1, -1)
        fw = (w_out[l].astype(BF16), row(ln1_g[l]), row(ln1_b[l]), w_ffn_up[l].astype(BF16),
              ffn_conv_w[l], row(ffn_conv_b[l]), w_ffn_down[l].astype(BF16), row(ln2_g[l]), row(ln2_b[l]))

        oa, s_new = _hgrn_prompt(xp, w_l[:, :4 * d_h], lb, ng)
        ob, re_new, im_new = _s5_prompt(xp, w_l[:, 4 * d_h:], s5p)
        xp, cache_new = _ffn_prompt(xp, oa, ob, fw, alpha)
        outs["hp"].append(s_new)
        outs["rp"].append(re_new.reshape(-1, groups, n_p))
        outs["ip"].append(im_new.reshape(-1, groups, n_p))
        outs["cp"].append(cache_new)

        oa, ob, s_new, re_new, im_new = _mix_sample(
            xs, w_l, lb, ng, state_hgrn[l], state_s5_re[l].reshape(n_dec, -1),
            state_s5_im[l].reshape(n_dec, -1), s5p)
        xs, cache_new = _ffn_sample(xs, oa, ob, cache_ffn_conv[l].reshape(n_dec, -1), fw, alpha)
        outs["hs"].append(s_new)
        outs["rs"].append(re_new.reshape(n_dec, groups, n_p))
        outs["is"].append(im_new.reshape(n_dec, groups, n_p))
        outs["cs"].append(cache_new.reshape(cache_ffn_conv.shape[1:]))

    st = {k: jnp.stack(v) for k, v in outs.items()}
    return (xp, xs.reshape(x_sample.shape), st["hp"], st["rp"], st["ip"], st["cp"],
            st["hs"], st["rs"], st["is"], st["cs"])
```

```python
import functools
import math

import jax
import jax.numpy as jnp
import numpy as np
from jax import lax
from jax.experimental import pallas as pl
from jax.experimental.pallas import tpu as pltpu

F32 = jnp.float32
BF16 = jnp.bfloat16

HEAD_DIM = 128
CHUNK = 64
S5_GROUP_CH = 16
S5_STATE = 64
LANES = 128
SUBLANES = 8
LN_EPS = 1e-5
RMS_EPS = 1e-6
F_FLOOR = 1e-20
SAFE_SPAN = 60.0
VMEM_LIMIT = 56 * 1024 * 1024

HGRN_TILE = 256
S5_TILE = 32
S5_PAD = 4
FFN_TILE = 512
FFN_SUB = 256
FFN_COLS = 256
FFN_ROWS = 32
SAMPLE_ROWS = 8


def _dot(a, b):
    return jnp.dot(a, b, preferred_element_type=F32)


def _dot_nt(a, b):
    return lax.dot_general(a, b, (((1,), (1,)), ((), ())), preferred_element_type=F32)


def _dot_tn(a, b):
    return lax.dot_general(a, b, (((0,), (0,)), ((), ())), preferred_element_type=F32)


def _split3(x):
    hi = x.astype(BF16)
    r1 = x - hi.astype(F32)
    mid = r1.astype(BF16)
    lo = (r1 - mid.astype(F32)).astype(BF16)
    return hi, mid, lo


def _dot3(m, parts):
    return _dot(m, parts[0]) + _dot(m, parts[1]) + _dot(m, parts[2])


def _layer_norm(x, g, b):
    mu = jnp.mean(x, axis=-1, keepdims=True)
    xc = x - mu
    var = jnp.mean(xc * xc, axis=-1, keepdims=True)
    return xc * lax.rsqrt(var + LN_EPS) * g + b


def _hgrn_gates(fz, lb):
    oml = 1.0 - lb
    f = lb + oml * jax.nn.sigmoid(fz)
    log_f = jnp.log(jnp.maximum(f, F_FLOOR))
    k = oml * jax.nn.sigmoid(-fz)
    return log_f, k


def _head_rms_gate(o, g, ng):
    heads = o.shape[-1] // HEAD_DIM
    outs = []
    for h in range(heads):
        oh = o[:, h * HEAD_DIM:(h + 1) * HEAD_DIM]
        ms = jnp.mean(oh * oh, axis=-1, keepdims=True)
        outs.append(oh * lax.rsqrt(ms + RMS_EPS))
    return jnp.concatenate(outs, axis=-1) * ng * jax.nn.sigmoid(g)


def _hgrn_prompt_kernel(x_ref, w_ref, lb_ref, ng_ref, tri_ref, lvl_ref, oa_ref, s_ref,
                        st_ref, sc_ref, *, tile, heads):
    t = pl.program_id(1)
    d_h = heads * HEAD_DIM

    @pl.when(t == 0)
    def _():
        st_ref[...] = jnp.zeros_like(st_ref)

    proj = _dot(x_ref[...].astype(BF16), w_ref[...])
    lb = lb_ref[...]
    ng = ng_ref[...]
    ri = lax.broadcasted_iota(jnp.int32, (CHUNK, CHUNK), 0)
    ci = lax.broadcasted_iota(jnp.int32, (CHUNK, CHUNK), 1)
    causal = ri >= ci
    row = lax.broadcasted_iota(jnp.int32, (CHUNK, 1), 0)

    for c in range(tile // CHUNK):
        r0 = c * CHUNK
        q = proj[r0:r0 + CHUNK, 0:d_h]
        fz = proj[r0:r0 + CHUNK, d_h:2 * d_h]
        v = proj[r0:r0 + CHUNK, 2 * d_h:3 * d_h]
        g = proj[r0:r0 + CHUNK, 3 * d_h:4 * d_h]
        log_f, k = _hgrn_gates(fz, lb)
        parts = _split3(log_f)
        b = _dot3(tri_ref[...], parts)
        mid = CHUNK // 2 - 1
        bm = b[mid:mid + 1, :]
        bl = b[CHUNK - 1:CHUNK, :]
        span_ok = jnp.min(jnp.minimum(bm, bl - bm)) >= -SAFE_SPAN

        @pl.when(span_ok)
        def _():
            qe = (q * jnp.exp(b - bm)).astype(BF16)
            ke = (k * jnp.exp(bm - b)).astype(BF16)
            for h in range(heads):
                sl = slice(h * HEAD_DIM, (h + 1) * HEAD_DIM)
                sc_ref[h] = jnp.where(causal, _dot_nt(qe[:, sl], ke[:, sl]), 0.0)

        @pl.when(jnp.logical_not(span_ok))
        def _():
            anchors = _dot3(lvl_ref[...], parts)
            qb = q.astype(BF16)
            kb = k.astype(BF16)
            for h in range(heads):
                sl = slice(h * HEAD_DIM, (h + 1) * HEAD_DIM)
                sc_ref[h] = jnp.where(ri == ci, _dot_nt(qb[:, sl], kb[:, sl]), 0.0)
            lvl = 0
            hs = CHUNK // 2
            while hs >= 1:
                a = anchors[lvl * CHUNK:(lvl + 1) * CHUNK, :]
                upper = (row & (2 * hs - 1)) >= hs
                eq = jnp.where(upper, jnp.exp(jnp.minimum(b - a, 0.0)), 0.0)
                ek = jnp.where(upper, 0.0, jnp.exp(jnp.minimum(a - b, 0.0)))
                qe = (q * eq).astype(BF16)
                ke = (k * ek).astype(BF16)
                shift = int(math.log2(2 * hs))
                same = (ri >> shift) == (ci >> shift)
                for h in range(heads):
                    sl = slice(h * HEAD_DIM, (h + 1) * HEAD_DIM)
                    sc_ref[h] += jnp.where(same, _dot_nt(qe[:, sl], ke[:, sl]), 0.0)
                lvl += 1
                hs //= 2

        qs = (q * jnp.exp(b)).astype(BF16)
        kl = (k * jnp.exp(bl - b)).astype(BF16)
        vb = v.astype(BF16)
        dec = jnp.exp(bl)
        outs = []
        for h in range(heads):
            sl = slice(h * HEAD_DIM, (h + 1) * HEAD_DIM)
            st = st_ref[h]
            o = _dot(sc_ref[h].astype(BF16), vb[:, sl]) + _dot_nt(qs[:, sl], st.astype(BF16))
            st_ref[h] = st * dec[:, sl] + _dot_tn(vb[:, sl], kl[:, sl])
            outs.append(o)
        o = jnp.concatenate(outs, axis=-1)
        oa_ref[r0:r0 + CHUNK, :] = _head_rms_gate(o, g, ng).astype(oa_ref.dtype)

    @pl.when(t == pl.num_programs(1) - 1)
    def _():
        for h in range(heads):
            s_ref[h] = st_ref[h].T


def _chunk_matrices():
    r = np.arange(CHUNK)
    tri = (r[None, :] <= r[:, None]).astype(np.float32)
    lvls = []
    hs = CHUNK // 2
    while hs >= 1:
        anchor = (r // (2 * hs)) * (2 * hs) + hs - 1
        lvls.append((r[None, :] <= anchor[:, None]).astype(np.float32))
        hs //= 2
    return jnp.asarray(tri, BF16), jnp.asarray(np.concatenate(lvls, axis=0), BF16)


def _hgrn_prompt(x, w_h, lb, ng):
    bsz, seq, d_model = x.shape
    d_h = lb.shape[-1]
    heads = d_h // HEAD_DIM
    tile = min(HGRN_TILE, seq)
    assert seq % tile == 0 and tile % CHUNK == 0
    tri, lvl = _chunk_matrices()
    const = lambda *shape: pl.BlockSpec(shape, lambda b, t: (0,) * len(shape))
    return pl.pallas_call(
        functools.partial(_hgrn_prompt_kernel, tile=tile, heads=heads),
        grid=(bsz, seq // tile),
        in_specs=[
            pl.BlockSpec((None, tile, d_model), lambda b, t: (b, t, 0)),
            const(d_model, 4 * d_h),
            const(1, d_h),
            const(1, d_h),
            const(CHUNK, CHUNK),
            const(lvl.shape[0], CHUNK),
        ],
        out_specs=[
            pl.BlockSpec((None, tile, d_h), lambda b, t: (b, t, 0)),
            pl.BlockSpec((None, heads, HEAD_DIM, HEAD_DIM), lambda b, t: (b, 0, 0, 0)),
        ],
        out_shape=[
            jax.ShapeDtypeStruct((bsz, seq, d_h), BF16),
            jax.ShapeDtypeStruct((bsz, heads, HEAD_DIM, HEAD_DIM), F32),
        ],
        scratch_shapes=[
            pltpu.VMEM((heads, HEAD_DIM, HEAD_DIM), F32),
            pltpu.VMEM((heads, CHUNK, CHUNK), F32),
        ],
        compiler_params=pltpu.CompilerParams(
            dimension_semantics=("arbitrary", "arbitrary"), vmem_limit_bytes=VMEM_LIMIT),
        name="hgrn_prompt",
    )(x, w_h, lb, ng, tri, lvl)


def _s5_input_drive(ub, bre_ref, bim_ref, store):
    kblocks = bre_ref.shape[0]
    slabs_per_block = bre_ref.shape[2] // LANES
    for m in range(kblocks):
        um = ub[:, m * LANES:(m + 1) * LANES]
        re = _dot(um, bre_ref[m])
        im = _dot(um, bim_ref[m])
        for s in range(slabs_per_block):
            store(m * slabs_per_block + s, re[:, s * LANES:(s + 1) * LANES], im[:, s * LANES:(s + 1) * LANES])


def _s5_readout(load, c_ref, d_s5):
    slabs = c_ref.shape[0]
    tiles = d_s5 // (2 * LANES)
    per_tile = slabs // tiles
    ys = []
    for n in range(tiles):
        acc = None
        for s in range(n * per_tile, (n + 1) * per_tile):
            part = _dot(load(s), c_ref[s])
            acc = part if acc is None else acc + part
        ys.append(acc)
    return jnp.concatenate(ys, axis=-1)


def _s5_output(y, u, d_ref, wglu_ref):
    y = y + d_ref[...] * u
    y = 0.5 * y * (1.0 + lax.erf(y * math.sqrt(0.5)))
    return y * jax.nn.sigmoid(_dot(y.astype(BF16), wglu_ref[...]))


def _s5_prompt_kernel(x_ref, wu_ref, bre_ref, bim_ref, c_ref, are_ref, aim_ref, d_ref, wglu_ref,
                      ob_ref, hre_ref, him_ref, xr_ref, xi_ref, y_ref, *, tile, bsz):
    i = pl.program_id(0)
    pitch = tile + S5_PAD
    slabs = xr_ref.shape[0]
    d_s5 = d_ref.shape[-1]

    @pl.when(i == 0)
    def _():
        hre_ref[...] = jnp.zeros_like(hre_ref)
        him_ref[...] = jnp.zeros_like(him_ref)
        xr_ref[...] = jnp.zeros_like(xr_ref)
        xi_ref[...] = jnp.zeros_like(xi_ref)

    x = x_ref[...].reshape(bsz * tile, x_ref.shape[-1])
    u = _dot(x.astype(BF16), wu_ref[...])

    def store(slab, re, im):
        for b in range(bsz):
            xr_ref[slab, b * pitch:b * pitch + tile, :] = re[b * tile:(b + 1) * tile, :]
            xi_ref[slab, b * pitch:b * pitch + tile, :] = im[b * tile:(b + 1) * tile, :]

    _s5_input_drive(u.astype(BF16), bre_ref, bim_ref, store)

    def step(tt, carry):
        hr, hi = carry
        nhr, nhi = [], []
        for s in range(slabs):
            ar = are_ref[:, s * LANES:(s + 1) * LANES]
            ai = aim_ref[:, s * LANES:(s + 1) * LANES]
            rows = pl.ds(tt, bsz, stride=pitch)
            nr = ar * hr[s] - ai * hi[s] + xr_ref[s, rows, :]
            ni = ar * hi[s] + ai * hr[s] + xi_ref[s, rows, :]
            xr_ref[s, rows, :] = nr
            xi_ref[s, rows, :] = ni
            nhr.append(nr)
            nhi.append(ni)
        return tuple(nhr), tuple(nhi)

    hr0 = tuple(hre_ref[:, s * LANES:(s + 1) * LANES] for s in range(slabs))
    hi0 = tuple(him_ref[:, s * LANES:(s + 1) * LANES] for s in range(slabs))
    hr, hi = lax.fori_loop(0, tile, step, (hr0, hi0))
    for s in range(slabs):
        hre_ref[:, s * LANES:(s + 1) * LANES] = hr[s]
        him_ref[:, s * LANES:(s + 1) * LANES] = hi[s]

    load = lambda s: jnp.concatenate([xr_ref[s], xi_ref[s]], axis=-1).astype(BF16)
    y_ref[...] = _s5_readout(load, c_ref, d_s5)
    y = jnp.concatenate([y_ref[b * pitch:b * pitch + tile, :] for b in range(bsz)], axis=0)
    o = _s5_output(y, u, d_ref, wglu_ref)
    ob_ref[...] = o.reshape(bsz, tile, d_s5).astype(ob_ref.dtype)


def _s5_prompt(x, w_u, s5p):
    bsz, seq, d_model = x.shape
    bre, bim, cblk, are, aim, d_skip, wglu = s5p
    d_s5 = d_skip.shape[-1]
    n_state = are.shape[-1]
    slabs = n_state // LANES
    tile = min(S5_TILE, seq)
    assert seq % tile == 0 and bsz == SUBLANES
    pitch = tile + S5_PAD
    const = lambda *shape: pl.BlockSpec(shape, lambda i: (0,) * len(shape))
    return pl.pallas_call(
        functools.partial(_s5_prompt_kernel, tile=tile, bsz=bsz),
        grid=(seq // tile,),
        in_specs=[
            pl.BlockSpec((bsz, tile, d_model), lambda i: (0, i, 0)),
            const(*w_u.shape), const(*bre.shape), const(*bim.shape), const(*cblk.shape),
            const(1, n_state), const(1, n_state), const(1, d_s5), const(*wglu.shape),
        ],
        out_specs=[
            pl.BlockSpec((bsz, tile, d_s5), lambda i: (0, i, 0)),
            const(bsz, n_state), const(bsz, n_state),
        ],
        out_shape=[
            jax.ShapeDtypeStruct((bsz, seq, d_s5), BF16),
            jax.ShapeDtypeStruct((bsz, n_state), F32),
            jax.ShapeDtypeStruct((bsz, n_state), F32),
        ],
        scratch_shapes=[
            pltpu.VMEM((slabs, bsz * pitch, LANES), F32),
            pltpu.VMEM((slabs, bsz * pitch, LANES), F32),
            pltpu.VMEM((bsz * pitch, d_s5), F32),
        ],
        compiler_params=pltpu.CompilerParams(
            dimension_semantics=("arbitrary",), vmem_limit_bytes=VMEM_LIMIT),
        name="s5_prompt",
    )(x, w_u, bre, bim, cblk, are, aim, d_skip, wglu)


def _mix_sample_kernel(x_ref, w_ref, lb_ref, ng_ref, s_ref, hre_ref, him_ref,
                       bre_ref, bim_ref, c_ref, are_ref, aim_ref, d_ref, wglu_ref,
                       oa_ref, ob_ref, so_ref, nre_ref, nim_ref,
                       proj_ref, orow_ref, *, heads):
    i = pl.program_id(0)
    d_h = heads * HEAD_DIM
    d_s5 = d_ref.shape[-1]
    slabs = are_ref.shape[-1] // LANES

    @pl.when(i == 0)
    def _():
        proj = _dot(x_ref[...].astype(BF16), w_ref[...])
        proj_ref[...] = proj
        u = proj[:, 4 * d_h:]

        def store(slab, re, im):
            sl = slice(slab * LANES, (slab + 1) * LANES)
            ar, ai = are_ref[:, sl], aim_ref[:, sl]
            hr, hi = hre_ref[:, sl], him_ref[:, sl]
            nre_ref[:, sl] = ar * hr - ai * hi + re
            nim_ref[:, sl] = ar * hi + ai * hr + im

        _s5_input_drive(u.astype(BF16), bre_ref, bim_ref, store)
        load = lambda s: jnp.concatenate(
            [nre_ref[:, s * LANES:(s + 1) * LANES], nim_ref[:, s * LANES:(s + 1) * LANES]],
            axis=-1).astype(BF16)
        y = _s5_readout(load, c_ref, d_s5)
        ob_ref[...] = _s5_output(y, u, d_ref, wglu_ref).astype(ob_ref.dtype)

    r0 = pl.multiple_of(i * SAMPLE_ROWS, SAMPLE_ROWS)
    rows = proj_ref[pl.ds(r0, SAMPLE_ROWS), :]
    q = rows[:, 0:d_h]
    fz = rows[:, d_h:2 * d_h]
    v = rows[:, 2 * d_h:3 * d_h]
    g = rows[:, 3 * d_h:4 * d_h]
    log_f, k = _hgrn_gates(fz, lb_ref[...])
    f = jnp.exp(log_f)
    per = heads * SAMPLE_ROWS
    pieces = [a[:, h * HEAD_DIM:(h + 1) * HEAD_DIM] for a in (q, f, k) for h in range(heads)]
    assert 3 * per <= HEAD_DIM
    pieces.append(jnp.zeros((HEAD_DIM - 3 * per, HEAD_DIM), F32))
    cols = jnp.concatenate(pieces, axis=0).T
    for h in range(heads):
        for r in range(SAMPLE_ROWS):
            j = h * SAMPLE_ROWS + r
            qc = cols[:, j:j + 1]
            fc = cols[:, per + j:per + j + 1]
            kc = cols[:, 2 * per + j:2 * per + j + 1]
            sn = fc * s_ref[r, h] + kc * v[r:r + 1, h * HEAD_DIM:(h + 1) * HEAD_DIM]
            so_ref[r, h] = sn
            orow_ref[r:r + 1, h * HEAD_DIM:(h + 1) * HEAD_DIM] = jnp.sum(qc * sn, axis=0, keepdims=True)
    oa_ref[pl.ds(r0, SAMPLE_ROWS), :] = _head_rms_gate(orow_ref[...], g, ng_ref[...]).astype(oa_ref.dtype)


def _mix_sample(x, w_in, lb, ng, s_h, h_re, h_im, s5p):
    n, d_model = x.shape
    bre, bim, cblk, are, aim, d_skip, wglu = s5p
    d_h = lb.shape[-1]
    heads = d_h // HEAD_DIM
    d_s5 = d_skip.shape[-1]
    n_state = are.shape[-1]
    assert n % SAMPLE_ROWS == 0
    const = lambda *shape: pl.BlockSpec(shape, lambda i: (0,) * len(shape))
    state_spec = pl.BlockSpec((SAMPLE_ROWS, heads, HEAD_DIM, HEAD_DIM), lambda i: (i, 0, 0, 0))
    return pl.pallas_call(
        functools.partial(_mix_sample_kernel, heads=heads),
        grid=(n // SAMPLE_ROWS,),
        in_specs=[
            const(n, d_model), const(*w_in.shape), const(1, d_h), const(1, d_h),
            state_spec, const(n, n_state), const(n, n_state),
            const(*bre.shape), const(*bim.shape), const(*cblk.shape),
            const(1, n_state), const(1, n_state), const(1, d_s5), const(*wglu.shape),
        ],
        out_specs=[
            const(n, d_h), const(n, d_s5), state_spec, const(n, n_state), const(n, n_state),
        ],
        out_shape=[
            jax.ShapeDtypeStruct((n, d_h), BF16),
            jax.ShapeDtypeStruct((n, d_s5), BF16),
            jax.ShapeDtypeStruct(s_h.shape, F32),
            jax.ShapeDtypeStruct((n, n_state), F32),
            jax.ShapeDtypeStruct((n, n_state), F32),
        ],
        scratch_shapes=[
            pltpu.VMEM((n, w_in.shape[1]), F32),
            pltpu.VMEM((SAMPLE_ROWS, d_h), F32),
        ],
        compiler_params=pltpu.CompilerParams(
            dimension_semantics=("arbitrary",), vmem_limit_bytes=VMEM_LIMIT),
        name="mix_sample",
    )(x, w_in, lb, ng, s_h, h_re, h_im, bre, bim, cblk, are, aim, d_skip, wglu)


def _mix_ln1(x, oa, ob, wout_ref, g1_ref, b1_ref, alpha):
    d_a = oa.shape[-1]
    mix = _dot(oa, wout_ref[0:d_a, :]) + _dot(ob, wout_ref[d_a:, :])
    return _layer_norm(alpha * x + mix, g1_ref[...], b1_ref[...])


def _ffn_blocks(x1b, wup_ref, cw_ref, cb_ref, wdown_ref, taps, d_ff):
    acc = None
    for j in range(d_ff // FFN_COLS):
        hs = []
        for base in (0, d_ff):
            cols = slice(base + j * FFN_COLS, base + (j + 1) * FFN_COLS)
            up = _dot(x1b, wup_ref[:, cols])
            m2, m1 = taps(up, cols)
            hs.append(cb_ref[:, cols] + m2 * cw_ref[0:1, cols] + m1 * cw_ref[1:2, cols]
                      + up * cw_ref[2:3, cols])
        val, gate = hs
        hh = (gate * jax.nn.sigmoid(gate)) * val
        part = _dot(hh.astype(BF16), wdown_ref[j * FFN_COLS:(j + 1) * FFN_COLS, :])
        acc = part if acc is None else acc + part
    return acc


def _ffn_prompt_kernel(x_ref, oa_ref, ob_ref, wout_ref, g1_ref, b1_ref, wup_ref, cw_ref, cb_ref,
                       wdown_ref, g2_ref, b2_ref, y_ref, cache_ref,
                       carry_ref, up_ref, hh_ref, acc_ref, x1_ref, x1b_ref, *, tile, alpha, d_ff):
    t = pl.program_id(1)
    lead = SUBLANES
    nb = d_ff // FFN_COLS
    sub = x1_ref.shape[1]
    subs = tile // sub
    slabs = FFN_COLS // LANES
    half_rows = FFN_ROWS // 2

    @pl.when(t == 0)
    def _():
        carry_ref[...] = jnp.zeros_like(carry_ref)

    def mix_ln1(s):
        rows = slice(s * sub, (s + 1) * sub)
        d_a = oa_ref.shape[-1]
        acc_ref[s] = _dot(oa_ref[rows, :], wout_ref[0:d_a, :]) + _dot(ob_ref[rows, :], wout_ref[d_a:, :])
        for r in range(0, sub, FFN_ROWS):
            rr = slice(r, r + FFN_ROWS)
            x1 = _layer_norm(alpha * x_ref[s * sub + r:s * sub + r + FFN_ROWS, :] + acc_ref[s, rr, :],
                             g1_ref[...], b1_ref[...])
            x1_ref[s, rr, :] = x1
            x1b_ref[s, rr, :] = x1.astype(BF16)

    def ln2(s):
        for r in range(0, sub, FFN_ROWS):
            rr = slice(r, r + FFN_ROWS)
            y_ref[s * sub + r:s * sub + r + FFN_ROWS, :] = _layer_norm(
                alpha * x1_ref[s, rr, :] + acc_ref[s, rr, :], g2_ref[...], b2_ref[...])

    def halves(j):
        return (slice(j * FFN_COLS, (j + 1) * FFN_COLS),
                slice(d_ff + j * FFN_COLS, d_ff + (j + 1) * FFN_COLS))

    def up_block(slot, s, j):
        for half, cols in enumerate(halves(j)):
            up = _dot(x1b_ref[s], wup_ref[:, cols])
            for i in range(slabs):
                lanes = slice(cols.start + i * LANES, cols.start + (i + 1) * LANES)
                up_ref[slot, half * slabs + i, lead:lead + sub, :] = up[:, i * LANES:(i + 1) * LANES]
                up_ref[slot, half * slabs + i, lead - 2:lead, :] = carry_ref[:, lanes]

    def conv_block(slot, s, j):
        for i in range(slabs):
            lanes = [slice(c.start + i * LANES, c.start + (i + 1) * LANES) for c in halves(j)]
            rep = lambda a: jnp.broadcast_to(a, (half_rows, LANES))
            prm = [(rep(cw_ref[0:1, l]), rep(cw_ref[1:2, l]), rep(cw_ref[2:3, l]), rep(cb_ref[:, l]))
                   for l in lanes]
            for r in range(0, sub, FFN_ROWS):
                res = []
                for half, (w0, w1, w2, cb) in enumerate(prm):
                    tap = lambda d: up_ref[slot, half * slabs + i, pl.ds(lead + r + d, half_rows, stride=2), :]
                    em, om, e, o = tap(-2), tap(-1), tap(0), tap(1)
                    res.append((cb + em * w0 + om * w1 + e * w2, cb + om * w0 + e * w1 + o * w2))
                (ve, vo), (ge, go) = res
                hh_ref[slot, i, pl.ds(r, half_rows, stride=2), :] = (ge * jax.nn.sigmoid(ge)) * ve
                hh_ref[slot, i, pl.ds(r + 1, half_rows, stride=2), :] = (go * jax.nn.sigmoid(go)) * vo
            for half, l in enumerate(lanes):
                carry_ref[:, l] = up_ref[slot, half * slabs + i, lead + sub - 2:lead + sub, :]

    def down_block(slot, s, j):
        hh = jnp.concatenate([hh_ref[slot, i] for i in range(slabs)], axis=-1).astype(BF16)
        part = _dot(hh, wdown_ref[j * FFN_COLS:(j + 1) * FFN_COLS, :])
        if j == 0:
            acc_ref[s] = part
        else:
            acc_ref[s] += part

    for s in range(subs):
        mix_ln1(s)
    blocks = [(s, j) for s in range(subs) for j in range(nb)]
    up_block(0, *blocks[0])
    for n, (s, j) in enumerate(blocks):
        if n + 1 < len(blocks):
            up_block((n + 1) % 2, *blocks[n + 1])
        conv_block(n % 2, s, j)
        down_block(n % 2, s, j)
        if j == nb - 1:
            ln2(s)
    cache_ref[...] = carry_ref[...]


def _ffn_sample_kernel(x_ref, oa_ref, ob_ref, cache_ref, wout_ref, g1_ref, b1_ref, wup_ref, cw_ref,
                       cb_ref, wdown_ref, g2_ref, b2_ref, y_ref, ncache_ref, *, alpha, d_ff):
    width = 2 * d_ff
    x1 = _mix_ln1(x_ref[...], oa_ref[...], ob_ref[...], wout_ref, g1_ref, b1_ref, alpha)

    def taps(up, cols):
        m2 = cache_ref[:, cols]
        m1 = cache_ref[:, slice(width + cols.start, width + cols.stop)]
        ncache_ref[:, cols] = m1
        ncache_ref[:, slice(width + cols.start, width + cols.stop)] = up
        return m2, m1

    ff = _ffn_blocks(x1.astype(BF16), wup_ref, cw_ref, cb_ref, wdown_ref, taps, d_ff)
    y_ref[...] = _layer_norm(alpha * x1 + ff, g2_ref[...], b2_ref[...])


def _resident(shape, grid_rank):
    zeros = (0,) * len(shape)
    index_map = {1: lambda i: zeros, 2: lambda b, t: zeros}[grid_rank]
    return pl.BlockSpec(shape, index_map, pipeline_mode=pl.Buffered(1))


def _ffn_prompt(x, oa, ob, fw, alpha):
    bsz, seq, d_model = x.shape
    wout, g1, b1, wup, cw, cb, wdown, g2, b2 = fw
    d_ff = wdown.shape[0]
    tile = min(FFN_TILE, seq)
    sub = min(FFN_SUB, tile)
    assert seq % tile == 0 and d_ff % FFN_COLS == 0 and tile % sub == 0 and sub % FFN_ROWS == 0
    subs = tile // sub
    res = lambda a: _resident(a.shape, 2)
    tok = lambda d: pl.BlockSpec((None, tile, d), lambda b, t: (b, t, 0))
    return pl.pallas_call(
        functools.partial(_ffn_prompt_kernel, tile=tile, alpha=alpha, d_ff=d_ff),
        grid=(bsz, seq // tile),
        in_specs=[tok(d_model), tok(oa.shape[-1]), tok(ob.shape[-1]),
                  res(wout), res(g1), res(b1), res(wup), res(cw), res(cb), res(wdown), res(g2), res(b2)],
        out_specs=[tok(d_model), pl.BlockSpec((None, 2, 2 * d_ff), lambda b, t: (b, 0, 0))],
        out_shape=[jax.ShapeDtypeStruct(x.shape, F32),
                   jax.ShapeDtypeStruct((bsz, 2, 2 * d_ff), F32)],
        scratch_shapes=[
            pltpu.VMEM((2, 2 * d_ff), F32),
            pltpu.VMEM((2, 2 * FFN_COLS // LANES, SUBLANES + sub, LANES), F32),
            pltpu.VMEM((2, FFN_COLS // LANES, sub, LANES), F32),
            pltpu.VMEM((subs, sub, d_model), F32),
            pltpu.VMEM((subs, sub, d_model), F32),
            pltpu.VMEM((subs, sub, d_model), BF16),
        ],
        compiler_params=pltpu.CompilerParams(
            dimension_semantics=("arbitrary", "arbitrary"), vmem_limit_bytes=VMEM_LIMIT),
        name="ffn_prompt",
    )(x, oa, ob, wout, g1, b1, wup, cw, cb, wdown, g2, b2)


def _ffn_sample(x, oa, ob, cache, fw, alpha):
    n, d_model = x.shape
    wout, g1, b1, wup, cw, cb, wdown, g2, b2 = fw
    d_ff = wdown.shape[0]
    args = (x, oa, ob, cache, wout, g1, b1, wup, cw, cb, wdown, g2, b2)
    return pl.pallas_call(
        functools.partial(_ffn_sample_kernel, alpha=alpha, d_ff=d_ff),
        grid=(1,),
        in_specs=[_resident(a.shape, 1) for a in args],
        out_specs=[pl.BlockSpec(x.shape, lambda i: (0, 0)), pl.BlockSpec(cache.shape, lambda i: (0, 0))],
        out_shape=[jax.ShapeDtypeStruct(x.shape, F32), jax.ShapeDtypeStruct(cache.shape, F32)],
        compiler_params=pltpu.CompilerParams(
            dimension_semantics=("arbitrary",), vmem_limit_bytes=VMEM_LIMIT),
        name="ffn_sample",
    )(*args)


def _s5_params(lam_re, lam_im, log_dt, b_re, b_im, c_re, c_im, d_skip, w_glu):
    groups, n_p = lam_re.shape
    lr = jnp.minimum(lam_re, -1e-4)
    li = lam_im
    dt = jnp.exp(log_dt)[:, None]
    mag = jnp.exp(lr * dt)
    ab_re = mag * jnp.cos(li * dt)
    ab_im = mag * jnp.sin(li * dt)
    den = lr * lr + li * li
    nr = ab_re - 1.0
    coef_re = (nr * lr + ab_im * li) / den
    coef_im = (ab_im * lr - nr * li) / den
    bb_re = coef_re[..., None] * b_re - coef_im[..., None] * b_im
    bb_im = coef_re[..., None] * b_im + coef_im[..., None] * b_re

    gpb = LANES // S5_GROUP_CH
    kblocks = groups // gpb
    eye = jnp.eye(gpb, dtype=F32)

    def b_blocks(bb):
        bb = bb.reshape(kblocks, gpb, n_p, S5_GROUP_CH)
        return jnp.einsum('mgpj,gh->mgjhp', bb, eye).reshape(kblocks, LANES, gpb * n_p).astype(BF16)

    gps = LANES // n_p
    slabs = groups // gps
    gpt = (2 * LANES) // S5_GROUP_CH
    pos = (jnp.arange(slabs)[:, None] * gps + jnp.arange(gps)[None, :]) % gpt
    onehot = jax.nn.one_hot(pos, gpt, dtype=F32)

    def c_half(c):
        c = c.reshape(slabs, gps, S5_GROUP_CH, n_p)
        return jnp.einsum('sgjp,sgh->sgphj', c, onehot).reshape(slabs, gps * n_p, gpt * S5_GROUP_CH)

    cblk = jnp.concatenate([c_half(c_re), -c_half(c_im)], axis=1).astype(BF16)
    return (b_blocks(bb_re), b_blocks(bb_im), cblk,
            ab_re.reshape(1, groups * n_p), ab_im.reshape(1, groups * n_p),
            d_skip.reshape(1, -1), w_glu.astype(BF16))


def kernel(x_prompt, x_sample, state_hgrn, state_s5_re, state_s5_im, cache_ffn_conv, w_in, hgrn_lb_logits, hgrn_norm_g, s5_lambda_re, s5_lambda_im, s5_log_dt, s5_b_re, s5_b_im, s5_c_re, s5_c_im, s5_d, w_glu, w_out, ln1_g, ln1_b, w_ffn_up, ffn_conv_w, ffn_conv_b, w_ffn_down, ln2_g, ln2_b):
    depth = w_in.shape[0]
    d_h = hgrn_lb_logits.shape[-1]
    heads = d_h // HEAD_DIM
    n_dec = x_sample.shape[0]
    groups, n_p = s5_lambda_re.shape[1:]
    alpha = (2 * depth) ** 0.25

    sm = jax.nn.softmax(hgrn_lb_logits.astype(F32), axis=0)
    lower_bounds = jnp.cumsum(sm, axis=0) - sm[0:1]

    xp = x_prompt
    xs = x_sample.reshape(n_dec, -1)
    outs = {k: [] for k in ("hp", "rp", "ip", "cp", "hs", "rs", "is", "cs")}
    for l in range(depth):
        w_l = w_in[l].astype(BF16)
        lb = lower_bounds[l].reshape(1, d_h)
        ng = jnp.tile(hgrn_norm_g[l], heads).reshape(1, d_h)
        s5p = _s5_params(s5_lambda_re[l], s5_lambda_im[l], s5_log_dt[l], s5_b_re[l], s5_b_im[l],
                         s5_c_re[l], s5_c_im[l], s5_d[l], w_glu[l])
        row = lambda a: a.reshape(1, -1)
        fw = (w_out[l].astype(BF16), row(ln1_g[l]), row(ln1_b[l]), w_ffn_up[l].astype(BF16),
              ffn_conv_w[l], row(ffn_conv_b[l]), w_ffn_down[l].astype(BF16), row(ln2_g[l]), row(ln2_b[l]))

        oa, s_new = _hgrn_prompt(xp, w_l[:, :4 * d_h], lb, ng)
        ob, re_new, im_new = _s5_prompt(xp, w_l[:, 4 * d_h:], s5p)
        xp, cache_new = _ffn_prompt(xp, oa, ob, fw, alpha)
        outs["hp"].append(s_new)
        outs["rp"].append(re_new.reshape(-1, groups, n_p))
        outs["ip"].append(im_new.reshape(-1, groups, n_p))
        outs["cp"].append(cache_new)

        oa, ob, s_new, re_new, im_new = _mix_sample(
            xs, w_l, lb, ng, state_hgrn[l], state_s5_re[l].reshape(n_dec, -1),
            state_s5_im[l].reshape(n_dec, -1), s5p)
        xs, cache_new = _ffn_sample(xs, oa, ob, cache_ffn_conv[l].reshape(n_dec, -1), fw, alpha)
        outs["hs"].append(s_new)
        outs["rs"].append(re_new.reshape(n_dec, groups, n_p))
        outs["is"].append(im_new.reshape(n_dec, groups, n_p))
        outs["cs"].append(cache_new.reshape(cache_ffn_conv.shape[1:]))

    st = {k: jnp.stack(v) for k, v in outs.items()}
    return (xp, xs.reshape(x_sample.shape), st["hp"], st["rp"], st["ip"], st["cp"],
            st["hs"], st["rs"], st["is"], st["cs"])
```

```python
import functools
import math

import jax
import jax.numpy as jnp
import numpy as np
from jax import lax
from jax.experimental import pallas as pl
from jax.experimental.pallas import tpu as pltpu

F32 = jnp.float32
BF16 = jnp.bfloat16

HEAD_DIM = 128
CHUNK = 64
S5_GROUP_CH = 16
S5_STATE = 64
LANES = 128
SUBLANES = 8
LN_EPS = 1e-5
RMS_EPS = 1e-6
F_FLOOR = 1e-20
SAFE_SPAN = 60.0
VMEM_LIMIT = 56 * 1024 * 1024

HGRN_TILE = 256
S5_TILE = 64
S5_SUB = 32
S5_PAD = 4
FFN_TILE = 512
FFN_SUB = 256
FFN_COLS = 256
FFN_ROWS = 32
SAMPLE_ROWS = 8


def _dot(a, b):
    return jnp.dot(a, b, preferred_element_type=F32)


def _dot_nt(a, b):
    return lax.dot_general(a, b, (((1,), (1,)), ((), ())), preferred_element_type=F32)


def _dot_tn(a, b):
    return lax.dot_general(a, b, (((0,), (0,)), ((), ())), preferred_element_type=F32)


def _split3(x):
    hi = x.astype(BF16)
    r1 = x - hi.astype(F32)
    mid = r1.astype(BF16)
    lo = (r1 - mid.astype(F32)).astype(BF16)
    return hi, mid, lo


def _dot3(m, parts):
    return _dot(m, parts[0]) + _dot(m, parts[1]) + _dot(m, parts[2])


def _layer_norm(x, g, b):
    mu = jnp.mean(x, axis=-1, keepdims=True)
    xc = x - mu
    var = jnp.mean(xc * xc, axis=-1, keepdims=True)
    return xc * lax.rsqrt(var + LN_EPS) * g + b


def _hgrn_gates(fz, lb):
    oml = 1.0 - lb
    f = lb + oml * jax.nn.sigmoid(fz)
    log_f = jnp.log(jnp.maximum(f, F_FLOOR))
    k = oml * jax.nn.sigmoid(-fz)
    return log_f, k


def _head_rms_gate(o, g, ng):
    heads = o.shape[-1] // HEAD_DIM
    outs = []
    for h in range(heads):
        oh = o[:, h * HEAD_DIM:(h + 1) * HEAD_DIM]
        ms = jnp.mean(oh * oh, axis=-1, keepdims=True)
        outs.append(oh * lax.rsqrt(ms + RMS_EPS))
    return jnp.concatenate(outs, axis=-1) * ng * jax.nn.sigmoid(g)


def _hgrn_prompt_kernel(x_ref, w_ref, lb_ref, ng_ref, tri_ref, lvl_ref, oa_ref, s_ref,
                        st_ref, st0_ref, proj_ref, *, tile, heads):
    t = pl.program_id(1)
    d_h = heads * HEAD_DIM

    @pl.when(t == 0)
    def _():
        st_ref[...] = jnp.zeros_like(st_ref)

    proj_ref[...] = _dot(x_ref[...].astype(BF16), w_ref[...])
    st0_ref[...] = st_ref[...]
    lb = lb_ref[...]
    ng = ng_ref[...]
    ri = lax.broadcasted_iota(jnp.int32, (CHUNK, CHUNK), 0)
    ci = lax.broadcasted_iota(jnp.int32, (CHUNK, CHUNK), 1)
    row = lax.broadcasted_iota(jnp.int32, (CHUNK, 1), 0)
    mid = CHUNK // 2 - 1

    def anchored_scores(q, k, b, parts):
        bm = b[mid:mid + 1, :]
        qe = (q * jnp.exp(jnp.minimum(b - bm, SAFE_SPAN))).astype(BF16)
        ke = (k * jnp.exp(jnp.minimum(bm - b, SAFE_SPAN))).astype(BF16)
        return [jnp.where(ri >= ci, _dot_nt(qe[:, h * HEAD_DIM:(h + 1) * HEAD_DIM],
                                             ke[:, h * HEAD_DIM:(h + 1) * HEAD_DIM]), 0.0)
                for h in range(heads)]

    def hierarchical_scores(q, k, b, parts):
        anchors = _dot3(lvl_ref[...], parts)
        qb = q.astype(BF16)
        kb = k.astype(BF16)
        sc = [jnp.where(ri == ci, _dot_nt(qb[:, h * HEAD_DIM:(h + 1) * HEAD_DIM],
                                          kb[:, h * HEAD_DIM:(h + 1) * HEAD_DIM]), 0.0)
              for h in range(heads)]
        lvl = 0
        hs = CHUNK // 2
        while hs >= 1:
            a = anchors[lvl * CHUNK:(lvl + 1) * CHUNK, :]
            upper = (row & (2 * hs - 1)) >= hs
            qe = (q * jnp.where(upper, jnp.exp(jnp.minimum(b - a, 0.0)), 0.0)).astype(BF16)
            ke = (k * jnp.where(upper, 0.0, jnp.exp(jnp.minimum(a - b, 0.0)))).astype(BF16)
            shift = int(math.log2(2 * hs))
            same = (ri >> shift) == (ci >> shift)
            for h in range(heads):
                sl = slice(h * HEAD_DIM, (h + 1) * HEAD_DIM)
                sc[h] = sc[h] + jnp.where(same, _dot_nt(qe[:, sl], ke[:, sl]), 0.0)
            lvl += 1
            hs //= 2
        return sc

    def run_tile(scores_fn):
        span = None
        for c in range(tile // CHUNK):
            rows = slice(c * CHUNK, (c + 1) * CHUNK)
            q = proj_ref[rows, 0:d_h]
            fz = proj_ref[rows, d_h:2 * d_h]
            v = proj_ref[rows, 2 * d_h:3 * d_h]
            g = proj_ref[rows, 3 * d_h:4 * d_h]
            log_f, k = _hgrn_gates(fz, lb)
            parts = _split3(log_f)
            b = _dot3(tri_ref[...], parts)
            bm = b[mid:mid + 1, :]
            bl = b[CHUNK - 1:CHUNK, :]
            worst = jnp.minimum(bm, bl - bm)
            span = worst if span is None else jnp.minimum(span, worst)
            sc = scores_fn(q, k, b, parts)
            qs = (q * jnp.exp(b)).astype(BF16)
            kl = (k * jnp.exp(bl - b)).astype(BF16)
            vb = v.astype(BF16)
            dec = jnp.exp(bl)
            outs = []
            for h in range(heads):
                sl = slice(h * HEAD_DIM, (h + 1) * HEAD_DIM)
                st = st_ref[h]
                outs.append(_dot(sc[h].astype(BF16), vb[:, sl]) + _dot_nt(qs[:, sl], st.astype(BF16)))
                st_ref[h] = st * dec[:, sl] + _dot_tn(vb[:, sl], kl[:, sl])
            o = jnp.concatenate(outs, axis=-1)
            oa_ref[rows, :] = _head_rms_gate(o, g, ng).astype(oa_ref.dtype)
        return jnp.min(span)

    span_ok = run_tile(anchored_scores) >= -SAFE_SPAN

    @pl.when(jnp.logical_not(span_ok))
    def _():
        st_ref[...] = st0_ref[...]
        run_tile(hierarchical_scores)

    @pl.when(t == pl.num_programs(1) - 1)
    def _():
        for h in range(heads):
            s_ref[h] = st_ref[h].T


def _chunk_matrices():
    r = np.arange(CHUNK)
    tri = (r[None, :] <= r[:, None]).astype(np.float32)
    lvls = []
    hs = CHUNK // 2
    while hs >= 1:
        anchor = (r // (2 * hs)) * (2 * hs) + hs - 1
        lvls.append((r[None, :] <= anchor[:, None]).astype(np.float32))
        hs //= 2
    return jnp.asarray(tri, BF16), jnp.asarray(np.concatenate(lvls, axis=0), BF16)


def _hgrn_prompt(x, w_in, layer, lb, ng):
    bsz, seq, d_model = x.shape
    d_h = lb.shape[-1]
    heads = d_h // HEAD_DIM
    tile = min(HGRN_TILE, seq)
    assert seq % tile == 0 and tile % CHUNK == 0
    tri, lvl = _chunk_matrices()
    const = lambda *shape: pl.BlockSpec(shape, lambda b, t: (0,) * len(shape))
    return pl.pallas_call(
        functools.partial(_hgrn_prompt_kernel, tile=tile, heads=heads),
        grid=(bsz, seq // tile),
        in_specs=[
            pl.BlockSpec((None, tile, d_model), lambda b, t: (b, t, 0)),
            pl.BlockSpec((None, d_model, 4 * d_h), lambda b, t: (layer, 0, 0)),
            const(1, d_h),
            const(1, d_h),
            const(CHUNK, CHUNK),
            const(lvl.shape[0], CHUNK),
        ],
        out_specs=[
            pl.BlockSpec((None, tile, d_h), lambda b, t: (b, t, 0)),
            pl.BlockSpec((None, heads, HEAD_DIM, HEAD_DIM), lambda b, t: (b, 0, 0, 0)),
        ],
        out_shape=[
            jax.ShapeDtypeStruct((bsz, seq, d_h), BF16),
            jax.ShapeDtypeStruct((bsz, heads, HEAD_DIM, HEAD_DIM), F32),
        ],
        scratch_shapes=[
            pltpu.VMEM((heads, HEAD_DIM, HEAD_DIM), F32),
            pltpu.VMEM((heads, HEAD_DIM, HEAD_DIM), F32),
            pltpu.VMEM((tile, 4 * d_h), F32),
        ],
        compiler_params=pltpu.CompilerParams(
            dimension_semantics=("arbitrary", "arbitrary"), vmem_limit_bytes=VMEM_LIMIT),
        name="hgrn_prompt",
    )(x, w_in, lb, ng, tri, lvl)


def _s5_input_drive(ub, bre_ref, bim_ref, store):
    kblocks = bre_ref.shape[0]
    slabs_per_block = bre_ref.shape[2] // LANES
    for m in range(kblocks):
        um = ub[:, m * LANES:(m + 1) * LANES]
        re = _dot(um, bre_ref[m])
        im = _dot(um, bim_ref[m])
        for s in range(slabs_per_block):
            store(m * slabs_per_block + s, re[:, s * LANES:(s + 1) * LANES], im[:, s * LANES:(s + 1) * LANES])


def _s5_readout(load, c_ref, d_s5):
    slabs = c_ref.shape[0]
    tiles = d_s5 // (2 * LANES)
    per_tile = slabs // tiles
    ys = []
    for n in range(tiles):
        acc = None
        for s in range(n * per_tile, (n + 1) * per_tile):
            part = _dot(load(s), c_ref[s])
            acc = part if acc is None else acc + part
        ys.append(acc)
    return jnp.concatenate(ys, axis=-1)


def _s5_output(y, u, d_ref, wglu_ref):
    y = y + d_ref[...] * u
    y = 0.5 * y * (1.0 + lax.erf(y * math.sqrt(0.5)))
    return y * jax.nn.sigmoid(_dot(y.astype(BF16), wglu_ref[...]))


def _s5_prompt_kernel(x_ref, wu_ref, bre_ref, bim_ref, c_ref, are_ref, aim_ref, d_ref, wglu_ref,
                      ob_ref, hre_ref, him_ref, xr_ref, xi_ref, u_ref, y_ref, *, tile, sub, bsz):
    i = pl.program_id(0)
    pitch = sub + S5_PAD
    subs = tile // sub
    slabs = xr_ref.shape[1]
    d_s5 = d_ref.shape[-1]

    @pl.when(i == 0)
    def _():
        hre_ref[...] = jnp.zeros_like(hre_ref)
        him_ref[...] = jnp.zeros_like(him_ref)
        xr_ref[...] = jnp.zeros_like(xr_ref)
        xi_ref[...] = jnp.zeros_like(xi_ref)

    def drive(k):
        x = x_ref[:, k * sub:(k + 1) * sub, :].reshape(bsz * sub, x_ref.shape[-1])
        u = _dot(x.astype(BF16), wu_ref[...])
        u_ref[k] = u

        def store(slab, re, im):
            for b in range(bsz):
                xr_ref[k, slab, b * pitch:b * pitch + sub, :] = re[b * sub:(b + 1) * sub, :]
                xi_ref[k, slab, b * pitch:b * pitch + sub, :] = im[b * sub:(b + 1) * sub, :]

        _s5_input_drive(u.astype(BF16), bre_ref, bim_ref, store)

    def scan(k, hr, hi):
        for tt in range(sub):
            rows = pl.ds(tt, bsz, stride=pitch)
            for s in range(slabs):
                ar = are_ref[:, s * LANES:(s + 1) * LANES]
                ai = aim_ref[:, s * LANES:(s + 1) * LANES]
                nr = ar * hr[s] - ai * hi[s] + xr_ref[k, s, rows, :]
                ni = ar * hi[s] + ai * hr[s] + xi_ref[k, s, rows, :]
                xr_ref[k, s, rows, :] = nr
                xi_ref[k, s, rows, :] = ni
                hr[s], hi[s] = nr, ni
        return hr, hi

    def readout(k):
        load = lambda s: jnp.concatenate([xr_ref[k, s], xi_ref[k, s]], axis=-1).astype(BF16)
        y_ref[k] = _s5_readout(load, c_ref, d_s5)
        y = jnp.concatenate([y_ref[k, b * pitch:b * pitch + sub, :] for b in range(bsz)], axis=0)
        o = _s5_output(y, u_ref[k], d_ref, wglu_ref)
        ob_ref[:, k * sub:(k + 1) * sub, :] = o.reshape(bsz, sub, d_s5).astype(ob_ref.dtype)

    for k in range(subs):
        drive(k)
    hr = [hre_ref[:, s * LANES:(s + 1) * LANES] for s in range(slabs)]
    hi = [him_ref[:, s * LANES:(s + 1) * LANES] for s in range(slabs)]
    for k in range(subs):
        hr, hi = scan(k, hr, hi)
        readout(k)
    for s in range(slabs):
        hre_ref[:, s * LANES:(s + 1) * LANES] = hr[s]
        him_ref[:, s * LANES:(s + 1) * LANES] = hi[s]


def _layer_block(a, layer, grid_rank):
    idx = (layer,) + (0,) * (a.ndim - 1)
    index_map = {1: lambda i: idx, 2: lambda b, t: idx}[grid_rank]
    return pl.BlockSpec((None,) + a.shape[1:], index_map)


def _s5_prompt(x, w_in, layer, s5p):
    bsz, seq, d_model = x.shape
    d_s5 = s5p[5].shape[-1]
    n_state = s5p[3].shape[-1]
    slabs = n_state // LANES
    tile = min(S5_TILE, seq)
    sub = min(S5_SUB, tile)
    assert seq % tile == 0 and tile % sub == 0 and bsz == SUBLANES and w_in.shape[-1] % d_s5 == 0
    subs = tile // sub
    pitch = sub + S5_PAD
    u_block = w_in.shape[-1] // d_s5 - 1
    const = lambda *shape: pl.BlockSpec(shape, lambda i: (0,) * len(shape))
    return pl.pallas_call(
        functools.partial(_s5_prompt_kernel, tile=tile, sub=sub, bsz=bsz),
        grid=(seq // tile,),
        in_specs=[
            pl.BlockSpec((bsz, tile, d_model), lambda i: (0, i, 0)),
            pl.BlockSpec((None, d_model, d_s5), lambda i: (layer, 0, u_block)),
        ] + [_layer_block(a, layer, 1) for a in s5p],
        out_specs=[
            pl.BlockSpec((bsz, tile, d_s5), lambda i: (0, i, 0)),
            const(bsz, n_state), const(bsz, n_state),
        ],
        out_shape=[
            jax.ShapeDtypeStruct((bsz, seq, d_s5), BF16),
            jax.ShapeDtypeStruct((bsz, n_state), F32),
            jax.ShapeDtypeStruct((bsz, n_state), F32),
        ],
        scratch_shapes=[
            pltpu.VMEM((subs, slabs, bsz * pitch, LANES), F32),
            pltpu.VMEM((subs, slabs, bsz * pitch, LANES), F32),
            pltpu.VMEM((subs, bsz * sub, d_s5), F32),
            pltpu.VMEM((subs, bsz * pitch, d_s5), F32),
        ],
        compiler_params=pltpu.CompilerParams(
            dimension_semantics=("arbitrary",), vmem_limit_bytes=VMEM_LIMIT),
        name="s5_prompt",
    )(x, w_in, *s5p)


def _mix_sample_kernel(x_ref, w_ref, lb_ref, ng_ref, s_ref, hre_ref, him_ref,
                       bre_ref, bim_ref, c_ref, are_ref, aim_ref, d_ref, wglu_ref, *rest, heads):
    oa_ref, ob_ref, so_ref, nre_ref, nim_ref, proj_ref, orow_ref = rest[-7:]
    i = pl.program_id(0)
    d_h = heads * HEAD_DIM
    d_s5 = d_ref.shape[-1]
    slabs = are_ref.shape[-1] // LANES

    @pl.when(i == 0)
    def _():
        proj = _dot(x_ref[...].astype(BF16), w_ref[...])
        proj_ref[...] = proj
        u = proj[:, 4 * d_h:]

        def store(slab, re, im):
            sl = slice(slab * LANES, (slab + 1) * LANES)
            ar, ai = are_ref[:, sl], aim_ref[:, sl]
            hr, hi = hre_ref[:, sl], him_ref[:, sl]
            nre_ref[:, sl] = ar * hr - ai * hi + re
            nim_ref[:, sl] = ar * hi + ai * hr + im

        _s5_input_drive(u.astype(BF16), bre_ref, bim_ref, store)
        load = lambda s: jnp.concatenate(
            [nre_ref[:, s * LANES:(s + 1) * LANES], nim_ref[:, s * LANES:(s + 1) * LANES]],
            axis=-1).astype(BF16)
        y = _s5_readout(load, c_ref, d_s5)
        ob_ref[...] = _s5_output(y, u, d_ref, wglu_ref).astype(ob_ref.dtype)

    r0 = pl.multiple_of(i * SAMPLE_ROWS, SAMPLE_ROWS)
    rows = proj_ref[pl.ds(r0, SAMPLE_ROWS), :]
    q = rows[:, 0:d_h]
    fz = rows[:, d_h:2 * d_h]
    v = rows[:, 2 * d_h:3 * d_h]
    g = rows[:, 3 * d_h:4 * d_h]
    log_f, k = _hgrn_gates(fz, lb_ref[...])
    f = jnp.exp(log_f)
    per = heads * SAMPLE_ROWS
    pieces = [a[:, h * HEAD_DIM:(h + 1) * HEAD_DIM] for a in (q, f, k) for h in range(heads)]
    assert 3 * per <= HEAD_DIM
    pieces.append(jnp.zeros((HEAD_DIM - 3 * per, HEAD_DIM), F32))
    cols = jnp.concatenate(pieces, axis=0).T
    for h in range(heads):
        for r in range(SAMPLE_ROWS):
            j = h * SAMPLE_ROWS + r
            qc = cols[:, j:j + 1]
            fc = cols[:, per + j:per + j + 1]
            kc = cols[:, 2 * per + j:2 * per + j + 1]
            sn = fc * s_ref[r, h] + kc * v[r:r + 1, h * HEAD_DIM:(h + 1) * HEAD_DIM]
            so_ref[r, h] = sn
            orow_ref[r:r + 1, h * HEAD_DIM:(h + 1) * HEAD_DIM] = jnp.sum(qc * sn, axis=0, keepdims=True)
    oa_ref[pl.ds(r0, SAMPLE_ROWS), :] = _head_rms_gate(orow_ref[...], g, ng_ref[...]).astype(oa_ref.dtype)


def _mix_sample(x, w_in, layer, lb, ng, s_all, h_re, h_im, s5p, s_new_prev):
    n, d_model = x.shape
    d_h = lb.shape[-1]
    heads = d_h // HEAD_DIM
    d_s5 = s5p[5].shape[-1]
    n_state = s5p[3].shape[-1]
    assert n % SAMPLE_ROWS == 0
    const = lambda *shape: pl.BlockSpec(shape, lambda i: (0,) * len(shape))
    state_spec = pl.BlockSpec((None, SAMPLE_ROWS, heads, HEAD_DIM, HEAD_DIM), lambda i: (layer, i, 0, 0, 0))
    args = [x, w_in, lb, ng, s_all, h_re, h_im, *s5p]
    in_specs = [const(n, d_model), _layer_block(w_in, layer, 1), const(1, d_h), const(1, d_h),
                state_spec, _layer_block(h_re, layer, 1), _layer_block(h_im, layer, 1)]
    in_specs += [_layer_block(a, layer, 1) for a in s5p]
    aliases = {}
    if s_new_prev is not None:
        aliases = {len(args): 2}
        args.append(s_new_prev)
        in_specs.append(pl.BlockSpec(memory_space=pl.ANY))
    return pl.pallas_call(
        functools.partial(_mix_sample_kernel, heads=heads),
        grid=(n // SAMPLE_ROWS,),
        in_specs=in_specs,
        out_specs=[
            const(n, d_h), const(n, d_s5), state_spec, const(n, n_state), const(n, n_state),
        ],
        out_shape=[
            jax.ShapeDtypeStruct((n, d_h), BF16),
            jax.ShapeDtypeStruct((n, d_s5), BF16),
            jax.ShapeDtypeStruct(s_all.shape, F32),
            jax.ShapeDtypeStruct((n, n_state), F32),
            jax.ShapeDtypeStruct((n, n_state), F32),
        ],
        scratch_shapes=[
            pltpu.VMEM((n, w_in.shape[-1]), F32),
            pltpu.VMEM((SAMPLE_ROWS, d_h), F32),
        ],
        input_output_aliases=aliases,
        compiler_params=pltpu.CompilerParams(
            dimension_semantics=("arbitrary",), vmem_limit_bytes=VMEM_LIMIT),
        name="mix_sample",
    )(*args)


def _mix_ln1(x, oa, ob, wout_ref, g1_ref, b1_ref, alpha):
    d_a = oa.shape[-1]
    mix = _dot(oa, wout_ref[0:d_a, :]) + _dot(ob, wout_ref[d_a:, :])
    return _layer_norm(alpha * x + mix, g1_ref[...], b1_ref[...])


def _ffn_blocks(x1b, wup_ref, cw_ref, cb_ref, wdown_ref, taps, d_ff):
    acc = None
    for j in range(d_ff // FFN_COLS):
        hs = []
        for base in (0, d_ff):
            cols = slice(base + j * FFN_COLS, base + (j + 1) * FFN_COLS)
            up = _dot(x1b, wup_ref[:, cols])
            m2, m1 = taps(up, cols)
            hs.append(cb_ref[:, cols] + m2 * cw_ref[0:1, cols] + m1 * cw_ref[1:2, cols]
                      + up * cw_ref[2:3, cols])
        val, gate = hs
        hh = (gate * jax.nn.sigmoid(gate)) * val
        part = _dot(hh.astype(BF16), wdown_ref[j * FFN_COLS:(j + 1) * FFN_COLS, :])
        acc = part if acc is None else acc + part
    return acc


def _ffn_prompt_kernel(x_ref, oa_ref, ob_ref, wout_ref, g1_ref, b1_ref, wup_ref, cw_ref, cb_ref,
                       wdown_ref, g2_ref, b2_ref, y_ref, cache_ref,
                       carry_ref, up_ref, hh_ref, acc_ref, x1_ref, x1b_ref, *, tile, alpha, d_ff):
    t = pl.program_id(1)
    lead = SUBLANES
    nb = d_ff // FFN_COLS
    sub = x1_ref.shape[1]
    subs = tile // sub
    slabs = FFN_COLS // LANES
    half_rows = FFN_ROWS // 2

    @pl.when(t == 0)
    def _():
        carry_ref[...] = jnp.zeros_like(carry_ref)

    def mix_ln1(s):
        rows = slice(s * sub, (s + 1) * sub)
        d_a = oa_ref.shape[-1]
        acc_ref[s] = _dot(oa_ref[rows, :], wout_ref[0:d_a, :]) + _dot(ob_ref[rows, :], wout_ref[d_a:, :])
        for r in range(0, sub, FFN_ROWS):
            rr = slice(r, r + FFN_ROWS)
            x1 = _layer_norm(alpha * x_ref[s * sub + r:s * sub + r + FFN_ROWS, :] + acc_ref[s, rr, :],
                             g1_ref[...], b1_ref[...])
            x1_ref[s, rr, :] = x1
            x1b_ref[s, rr, :] = x1.astype(BF16)

    def ln2(s):
        for r in range(0, sub, FFN_ROWS):
            rr = slice(r, r + FFN_ROWS)
            y_ref[s * sub + r:s * sub + r + FFN_ROWS, :] = _layer_norm(
                alpha * x1_ref[s, rr, :] + acc_ref[s, rr, :], g2_ref[...], b2_ref[...])

    def halves(j):
        return (slice(j * FFN_COLS, (j + 1) * FFN_COLS),
                slice(d_ff + j * FFN_COLS, d_ff + (j + 1) * FFN_COLS))

    def up_block(slot, s, j):
        for half, cols in enumerate(halves(j)):
            up = _dot(x1b_ref[s], wup_ref[:, cols])
            for i in range(slabs):
                lanes = slice(cols.start + i * LANES, cols.start + (i + 1) * LANES)
                up_ref[slot, half * slabs + i, lead:lead + sub, :] = up[:, i * LANES:(i + 1) * LANES]
                up_ref[slot, half * slabs + i, lead - 2:lead, :] = carry_ref[:, lanes]

    def conv_block(slot, s, j):
        for i in range(slabs):
            lanes = [slice(c.start + i * LANES, c.start + (i + 1) * LANES) for c in halves(j)]
            rep = lambda a: jnp.broadcast_to(a, (half_rows, LANES))
            prm = [(rep(cw_ref[0:1, l]), rep(cw_ref[1:2, l]), rep(cw_ref[2:3, l]), rep(cb_ref[:, l]))
                   for l in lanes]
            for r in range(0, sub, FFN_ROWS):
                res = []
                for half, (w0, w1, w2, cb) in enumerate(prm):
                    tap = lambda d: up_ref[slot, half * slabs + i, pl.ds(lead + r + d, half_rows, stride=2), :]
                    em, om, e, o = tap(-2), tap(-1), tap(0), tap(1)
                    res.append((cb + em * w0 + om * w1 + e * w2, cb + om * w0 + e * w1 + o * w2))
                (ve, vo), (ge, go) = res
                hh_ref[slot, i, pl.ds(r, half_rows, stride=2), :] = (ge * jax.nn.sigmoid(ge)) * ve
                hh_ref[slot, i, pl.ds(r + 1, half_rows, stride=2), :] = (go * jax.nn.sigmoid(go)) * vo
            for half, l in enumerate(lanes):
                carry_ref[:, l] = up_ref[slot, half * slabs + i, lead + sub - 2:lead + sub, :]

    def down_block(slot, s, j):
        hh = jnp.concatenate([hh_ref[slot, i] for i in range(slabs)], axis=-1).astype(BF16)
        part = _dot(hh, wdown_ref[j * FFN_COLS:(j + 1) * FFN_COLS, :])
        if j == 0:
            acc_ref[s] = part
        else:
            acc_ref[s] += part

    for s in range(subs):
        mix_ln1(s)
    blocks = [(s, j) for s in range(subs) for j in range(nb)]
    up_block(0, *blocks[0])
    for n, (s, j) in enumerate(blocks):
        if n + 1 < len(blocks):
            up_block((n + 1) % 2, *blocks[n + 1])
        conv_block(n % 2, s, j)
        down_block(n % 2, s, j)
        if j == nb - 1:
            ln2(s)
    cache_ref[...] = carry_ref[...]


def _ffn_sample_kernel(x_ref, oa_ref, ob_ref, cache_ref, wout_ref, g1_ref, b1_ref, wup_ref, cw_ref,
                       cb_ref, wdown_ref, g2_ref, b2_ref, *rest, alpha, d_ff):
    y_ref, ncache_ref = rest[-2:]
    width = 2 * d_ff
    x1 = _mix_ln1(x_ref[...], oa_ref[...], ob_ref[...], wout_ref, g1_ref, b1_ref, alpha)

    def taps(up, cols):
        m2 = cache_ref[:, cols]
        m1 = cache_ref[:, slice(width + cols.start, width + cols.stop)]
        ncache_ref[:, cols] = m1
        ncache_ref[:, slice(width + cols.start, width + cols.stop)] = up
        return m2, m1

    ff = _ffn_blocks(x1.astype(BF16), wup_ref, cw_ref, cb_ref, wdown_ref, taps, d_ff)
    y_ref[...] = _layer_norm(alpha * x1 + ff, g2_ref[...], b2_ref[...])


def _resident(a, layer, grid_rank):
    idx = (layer,) + (0,) * (a.ndim - 1)
    index_map = {1: lambda i: idx, 2: lambda b, t: idx}[grid_rank]
    return pl.BlockSpec((None,) + a.shape[1:], index_map, pipeline_mode=pl.Buffered(1))


def _ffn_prompt(x, oa, ob, fw, layer, alpha):
    bsz, seq, d_model = x.shape
    d_ff = fw[6].shape[1]
    tile = min(FFN_TILE, seq)
    sub = min(FFN_SUB, tile)
    assert seq % tile == 0 and d_ff % FFN_COLS == 0 and tile % sub == 0 and sub % FFN_ROWS == 0
    subs = tile // sub
    tok = lambda d: pl.BlockSpec((None, tile, d), lambda b, t: (b, t, 0))
    return pl.pallas_call(
        functools.partial(_ffn_prompt_kernel, tile=tile, alpha=alpha, d_ff=d_ff),
        grid=(bsz, seq // tile),
        in_specs=[tok(d_model), tok(oa.shape[-1]), tok(ob.shape[-1])] + [_resident(a, layer, 2) for a in fw],
        out_specs=[tok(d_model), pl.BlockSpec((None, 2, 2 * d_ff), lambda b, t: (b, 0, 0))],
        out_shape=[jax.ShapeDtypeStruct(x.shape, F32),
                   jax.ShapeDtypeStruct((bsz, 2, 2 * d_ff), F32)],
        scratch_shapes=[
            pltpu.VMEM((2, 2 * d_ff), F32),
            pltpu.VMEM((2, 2 * FFN_COLS // LANES, SUBLANES + sub, LANES), F32),
            pltpu.VMEM((2, FFN_COLS // LANES, sub, LANES), F32),
            pltpu.VMEM((subs, sub, d_model), F32),
            pltpu.VMEM((subs, sub, d_model), F32),
            pltpu.VMEM((subs, sub, d_model), BF16),
        ],
        compiler_params=pltpu.CompilerParams(
            dimension_semantics=("arbitrary", "arbitrary"), vmem_limit_bytes=VMEM_LIMIT),
        name="ffn_prompt",
    )(x, oa, ob, *fw)


def _ffn_sample(x, oa, ob, cache_all, fw, layer, alpha, cache_new_prev):
    n, d_model = x.shape
    d_ff = fw[6].shape[1]
    whole = lambda a: pl.BlockSpec(a.shape, lambda i: (0,) * a.ndim, pipeline_mode=pl.Buffered(1))
    args = [x, oa, ob, cache_all, *fw]
    in_specs = [whole(x), whole(oa), whole(ob), _resident(cache_all, layer, 1)]
    in_specs += [_resident(a, layer, 1) for a in fw]
    aliases = {}
    if cache_new_prev is not None:
        aliases = {len(args): 1}
        args.append(cache_new_prev)
        in_specs.append(pl.BlockSpec(memory_space=pl.ANY))
    return pl.pallas_call(
        functools.partial(_ffn_sample_kernel, alpha=alpha, d_ff=d_ff),
        grid=(1,),
        in_specs=in_specs,
        out_specs=[pl.BlockSpec(x.shape, lambda i: (0, 0)),
                   pl.BlockSpec((None,) + cache_all.shape[1:], lambda i: (layer, 0, 0))],
        out_shape=[jax.ShapeDtypeStruct(x.shape, F32), jax.ShapeDtypeStruct(cache_all.shape, F32)],
        input_output_aliases=aliases,
        compiler_params=pltpu.CompilerParams(
            dimension_semantics=("arbitrary",), vmem_limit_bytes=VMEM_LIMIT),
        name="ffn_sample",
    )(*args)


def _s5_params(lam_re, lam_im, log_dt, b_re, b_im, c_re, c_im, d_skip, w_glu):
    groups, n_p = lam_re.shape
    lr = jnp.minimum(lam_re, -1e-4)
    li = lam_im
    dt = jnp.exp(log_dt)[:, None]
    mag = jnp.exp(lr * dt)
    ab_re = mag * jnp.cos(li * dt)
    ab_im = mag * jnp.sin(li * dt)
    den = lr * lr + li * li
    nr = ab_re - 1.0
    coef_re = (nr * lr + ab_im * li) / den
    coef_im = (ab_im * lr - nr * li) / den
    bb_re = coef_re[..., None] * b_re - coef_im[..., None] * b_im
    bb_im = coef_re[..., None] * b_im + coef_im[..., None] * b_re

    gpb = LANES // S5_GROUP_CH
    kblocks = groups // gpb
    eye = jnp.eye(gpb, dtype=F32)

    def b_blocks(bb):
        bb = bb.reshape(kblocks, gpb, n_p, S5_GROUP_CH)
        return jnp.einsum('mgpj,gh->mgjhp', bb, eye).reshape(kblocks, LANES, gpb * n_p).astype(BF16)

    gps = LANES // n_p
    slabs = groups // gps
    gpt = (2 * LANES) // S5_GROUP_CH
    pos = (jnp.arange(slabs)[:, None] * gps + jnp.arange(gps)[None, :]) % gpt
    onehot = jax.nn.one_hot(pos, gpt, dtype=F32)

    def c_half(c):
        c = c.reshape(slabs, gps, S5_GROUP_CH, n_p)
        return jnp.einsum('sgjp,sgh->sgphj', c, onehot).reshape(slabs, gps * n_p, gpt * S5_GROUP_CH)

    cblk = jnp.concatenate([c_half(c_re), -c_half(c_im)], axis=1).astype(BF16)
    return (b_blocks(bb_re), b_blocks(bb_im), cblk,
            ab_re.reshape(1, groups * n_p), ab_im.reshape(1, groups * n_p),
            d_skip.reshape(1, -1), w_glu.astype(BF16))


def kernel(x_prompt, x_sample, state_hgrn, state_s5_re, state_s5_im, cache_ffn_conv, w_in, hgrn_lb_logits, hgrn_norm_g, s5_lambda_re, s5_lambda_im, s5_log_dt, s5_b_re, s5_b_im, s5_c_re, s5_c_im, s5_d, w_glu, w_out, ln1_g, ln1_b, w_ffn_up, ffn_conv_w, ffn_conv_b, w_ffn_down, ln2_g, ln2_b):
    depth = w_in.shape[0]
    d_h = hgrn_lb_logits.shape[-1]
    heads = d_h // HEAD_DIM
    n_dec = x_sample.shape[0]
    groups, n_p = s5_lambda_re.shape[1:]
    alpha = (2 * depth) ** 0.25

    sm = jax.nn.softmax(hgrn_lb_logits.astype(F32), axis=0)
    lower_bounds = jnp.cumsum(sm, axis=0) - sm[0:1]

    w_in_b = w_in.astype(BF16)
    s5p = jax.vmap(_s5_params)(s5_lambda_re, s5_lambda_im, s5_log_dt, s5_b_re, s5_b_im,
                               s5_c_re, s5_c_im, s5_d, w_glu)
    rows = lambda a: a.reshape(depth, 1, -1)
    fw = (w_out.astype(BF16), rows(ln1_g), rows(ln1_b), w_ffn_up.astype(BF16),
          ffn_conv_w, rows(ffn_conv_b), w_ffn_down.astype(BF16), rows(ln2_g), rows(ln2_b))
    h_re = state_s5_re.reshape(depth, n_dec, groups * n_p)
    h_im = state_s5_im.reshape(depth, n_dec, groups * n_p)
    cache_all = cache_ffn_conv.reshape(depth, n_dec, -1)

    xp = x_prompt
    xs = x_sample.reshape(n_dec, -1)
    outs = {k: [] for k in ("hp", "rp", "ip", "cp", "rs", "is")}
    hs_new = None
    cs_new = None
    for l in range(depth):
        lb = lower_bounds[l].reshape(1, d_h)
        ng = jnp.tile(hgrn_norm_g[l], heads).reshape(1, d_h)

        oa, s_new = _hgrn_prompt(xp, w_in_b, l, lb, ng)
        ob, re_new, im_new = _s5_prompt(xp, w_in_b, l, s5p)
        xp, cache_new = _ffn_prompt(xp, oa, ob, fw, l, alpha)
        outs["hp"].append(s_new)
        outs["rp"].append(re_new.reshape(-1, groups, n_p))
        outs["ip"].append(im_new.reshape(-1, groups, n_p))
        outs["cp"].append(cache_new)

        oa, ob, hs_new, re_new, im_new = _mix_sample(xs, w_in_b, l, lb, ng, state_hgrn, h_re, h_im, s5p, hs_new)
        xs, cs_new = _ffn_sample(xs, oa, ob, cache_all, fw, l, alpha, cs_new)
        outs["rs"].append(re_new.reshape(n_dec, groups, n_p))
        outs["is"].append(im_new.reshape(n_dec, groups, n_p))

    st = {k: jnp.stack(v) for k, v in outs.items()}
    return (xp, xs.reshape(x_sample.shape), st["hp"], st["rp"], st["ip"], st["cp"],
            hs_new, st["rs"], st["is"], cs_new.reshape(cache_ffn_conv.shape))
```

```python
import functools
import math

import jax
import jax.numpy as jnp
import numpy as np
from jax import lax
from jax.experimental import pallas as pl
from jax.experimental.pallas import tpu as pltpu

F32 = jnp.float32
BF16 = jnp.bfloat16

HEAD_DIM = 128
CHUNK = 64
S5_GROUP_CH = 16
S5_STATE = 64
LANES = 128
SUBLANES = 8
LN_EPS = 1e-5
RMS_EPS = 1e-6
F_FLOOR = 1e-20
SAFE_SPAN = 60.0
VMEM_LIMIT = 56 * 1024 * 1024

HGRN_TILE = 512
HGRN_SUB = 256
S5_TILE = 64
S5_SUB = 32
S5_PAD = 4
FFN_TILE = 512
FFN_SUB = 256
FFN_COLS = 256
FFN_ROWS = 32
SAMPLE_ROWS = 8


def _dot(a, b):
    return jnp.dot(a, b, preferred_element_type=F32)


def _dot_nt(a, b):
    return lax.dot_general(a, b, (((1,), (1,)), ((), ())), preferred_element_type=F32)


def _dot_tn(a, b):
    return lax.dot_general(a, b, (((0,), (0,)), ((), ())), preferred_element_type=F32)


def _split3(x):
    hi = x.astype(BF16)
    r1 = x - hi.astype(F32)
    mid = r1.astype(BF16)
    lo = (r1 - mid.astype(F32)).astype(BF16)
    return hi, mid, lo


def _dot3(m, parts):
    return _dot(m, parts[0]) + _dot(m, parts[1]) + _dot(m, parts[2])


def _layer_norm(x, g, b):
    mu = jnp.mean(x, axis=-1, keepdims=True)
    xc = x - mu
    var = jnp.mean(xc * xc, axis=-1, keepdims=True)
    return xc * lax.rsqrt(var + LN_EPS) * g + b


def _hgrn_gates(fz, lb):
    oml = 1.0 - lb
    f = lb + oml * jax.nn.sigmoid(fz)
    log_f = jnp.log(jnp.maximum(f, F_FLOOR))
    k = oml * jax.nn.sigmoid(-fz)
    return log_f, k


def _head_rms_gate(o, g, ng):
    heads = o.shape[-1] // HEAD_DIM
    outs = []
    for h in range(heads):
        oh = o[:, h * HEAD_DIM:(h + 1) * HEAD_DIM]
        ms = jnp.mean(oh * oh, axis=-1, keepdims=True)
        outs.append(oh * lax.rsqrt(ms + RMS_EPS))
    return jnp.concatenate(outs, axis=-1) * ng * jax.nn.sigmoid(g)


def _hgrn_prompt_kernel(x_ref, w_ref, lb_ref, ng_ref, tri_ref, lvl_ref, oa_ref, s_ref,
                        st_ref, st0_ref, proj_ref, *, tile, heads):
    t = pl.program_id(1)
    d_h = heads * HEAD_DIM

    @pl.when(t == 0)
    def _():
        st_ref[...] = jnp.zeros_like(st_ref)

    sub = min(HGRN_SUB, tile)

    pieces = sub // CHUNK
    piece_cols = 4 * d_h // pieces

    def project(k, j):
        rows = slice(k * sub, (k + 1) * sub)
        cols = slice(j * piece_cols, (j + 1) * piece_cols)
        proj_ref[rows, cols] = _dot(x_ref[rows, :].astype(BF16), w_ref[:, cols])

    st0_ref[...] = st_ref[...]
    lb = lb_ref[...]
    ng = ng_ref[...]
    ri = lax.broadcasted_iota(jnp.int32, (CHUNK, CHUNK), 0)
    ci = lax.broadcasted_iota(jnp.int32, (CHUNK, CHUNK), 1)
    row = lax.broadcasted_iota(jnp.int32, (CHUNK, 1), 0)
    mid = CHUNK // 2 - 1

    def anchored_scores(q, k, b, parts):
        bm = b[mid:mid + 1, :]
        qe = (q * jnp.exp(jnp.minimum(b - bm, SAFE_SPAN))).astype(BF16)
        ke = (k * jnp.exp(jnp.minimum(bm - b, SAFE_SPAN))).astype(BF16)
        return [jnp.where(ri >= ci, _dot_nt(qe[:, h * HEAD_DIM:(h + 1) * HEAD_DIM],
                                             ke[:, h * HEAD_DIM:(h + 1) * HEAD_DIM]), 0.0)
                for h in range(heads)]

    def hierarchical_scores(q, k, b, parts):
        anchors = _dot3(lvl_ref[...], parts)
        qb = q.astype(BF16)
        kb = k.astype(BF16)
        sc = [jnp.where(ri == ci, _dot_nt(qb[:, h * HEAD_DIM:(h + 1) * HEAD_DIM],
                                          kb[:, h * HEAD_DIM:(h + 1) * HEAD_DIM]), 0.0)
              for h in range(heads)]
        lvl = 0
        hs = CHUNK // 2
        while hs >= 1:
            a = anchors[lvl * CHUNK:(lvl + 1) * CHUNK, :]
            upper = (row & (2 * hs - 1)) >= hs
            qe = (q * jnp.where(upper, jnp.exp(jnp.minimum(b - a, 0.0)), 0.0)).astype(BF16)
            ke = (k * jnp.where(upper, 0.0, jnp.exp(jnp.minimum(a - b, 0.0)))).astype(BF16)
            shift = int(math.log2(2 * hs))
            same = (ri >> shift) == (ci >> shift)
            for h in range(heads):
                sl = slice(h * HEAD_DIM, (h + 1) * HEAD_DIM)
                sc[h] = sc[h] + jnp.where(same, _dot_nt(qe[:, sl], ke[:, sl]), 0.0)
            lvl += 1
            hs //= 2
        return sc

    def chunk(rows, scores_fn):
        q = proj_ref[rows, 0:d_h]
        fz = proj_ref[rows, d_h:2 * d_h]
        v = proj_ref[rows, 2 * d_h:3 * d_h]
        g = proj_ref[rows, 3 * d_h:4 * d_h]
        log_f, k = _hgrn_gates(fz, lb)
        parts = _split3(log_f)
        b = _dot3(tri_ref[...], parts)
        bm = b[mid:mid + 1, :]
        bl = b[CHUNK - 1:CHUNK, :]
        sc = scores_fn(q, k, b, parts)
        qs = (q * jnp.exp(b)).astype(BF16)
        kl = (k * jnp.exp(bl - b)).astype(BF16)
        vb = v.astype(BF16)
        dec = jnp.exp(bl)
        outs = []
        for h in range(heads):
            sl = slice(h * HEAD_DIM, (h + 1) * HEAD_DIM)
            st = st_ref[h]
            outs.append(_dot(sc[h].astype(BF16), vb[:, sl]) + _dot_nt(qs[:, sl], st.astype(BF16)))
            st_ref[h] = st * dec[:, sl] + _dot_tn(vb[:, sl], kl[:, sl])
        o = jnp.concatenate(outs, axis=-1)
        oa_ref[rows, :] = _head_rms_gate(o, g, ng).astype(oa_ref.dtype)
        return jnp.minimum(bm, bl - bm)

    span = None
    for j in range(pieces):
        project(0, j)
    for c in range(tile // CHUNK):
        k, j = divmod(c, pieces)
        if (k + 1) * sub < tile:
            project(k + 1, j)
        worst = chunk(slice(c * CHUNK, (c + 1) * CHUNK), anchored_scores)
        span = worst if span is None else jnp.minimum(span, worst)
    span_ok = jnp.min(span) >= -SAFE_SPAN

    @pl.when(jnp.logical_not(span_ok))
    def _():
        st_ref[...] = st0_ref[...]

        def redo(c, carry):
            chunk(pl.ds(pl.multiple_of(c * CHUNK, CHUNK), CHUNK), hierarchical_scores)
            return carry

        lax.fori_loop(0, tile // CHUNK, redo, 0)

    @pl.when(t == pl.num_programs(1) - 1)
    def _():
        for h in range(heads):
            s_ref[h] = st_ref[h].T


def _chunk_matrices():
    r = np.arange(CHUNK)
    tri = (r[None, :] <= r[:, None]).astype(np.float32)
    lvls = []
    hs = CHUNK // 2
    while hs >= 1:
        anchor = (r // (2 * hs)) * (2 * hs) + hs - 1
        lvls.append((r[None, :] <= anchor[:, None]).astype(np.float32))
        hs //= 2
    return jnp.asarray(tri, BF16), jnp.asarray(np.concatenate(lvls, axis=0), BF16)


def _hgrn_prompt(x, w_in, layer, lb, ng):
    bsz, seq, d_model = x.shape
    d_h = lb.shape[-1]
    heads = d_h // HEAD_DIM
    tile = min(HGRN_TILE, seq)
    assert seq % tile == 0 and tile % CHUNK == 0
    tri, lvl = _chunk_matrices()
    const = lambda *shape: pl.BlockSpec(shape, lambda b, t: (0,) * len(shape))
    return pl.pallas_call(
        functools.partial(_hgrn_prompt_kernel, tile=tile, heads=heads),
        grid=(bsz, seq // tile),
        in_specs=[
            pl.BlockSpec((None, tile, d_model), lambda b, t: (b, t, 0)),
            pl.BlockSpec((None, d_model, 4 * d_h), lambda b, t: (layer, 0, 0)),
            const(1, d_h),
            const(1, d_h),
            const(CHUNK, CHUNK),
            const(lvl.shape[0], CHUNK),
        ],
        out_specs=[
            pl.BlockSpec((None, tile, d_h), lambda b, t: (b, t, 0)),
            pl.BlockSpec((None, heads, HEAD_DIM, HEAD_DIM), lambda b, t: (b, 0, 0, 0)),
        ],
        out_shape=[
            jax.ShapeDtypeStruct((bsz, seq, d_h), BF16),
            jax.ShapeDtypeStruct((bsz, heads, HEAD_DIM, HEAD_DIM), F32),
        ],
        scratch_shapes=[
            pltpu.VMEM((heads, HEAD_DIM, HEAD_DIM), F32),
            pltpu.VMEM((heads, HEAD_DIM, HEAD_DIM), F32),
            pltpu.VMEM((tile, 4 * d_h), F32),
        ],
        compiler_params=pltpu.CompilerParams(
            dimension_semantics=("arbitrary", "arbitrary"), vmem_limit_bytes=VMEM_LIMIT),
        name="hgrn_prompt",
    )(x, w_in, lb, ng, tri, lvl)


def _s5_input_drive(ub, bre_ref, bim_ref, store):
    kblocks = bre_ref.shape[0]
    slabs_per_block = bre_ref.shape[2] // LANES
    for m in range(kblocks):
        um = ub[:, m * LANES:(m + 1) * LANES]
        re = _dot(um, bre_ref[m])
        im = _dot(um, bim_ref[m])
        for s in range(slabs_per_block):
            store(m * slabs_per_block + s, re[:, s * LANES:(s + 1) * LANES], im[:, s * LANES:(s + 1) * LANES])


def _s5_readout(load, c_ref, d_s5):
    slabs = c_ref.shape[0]
    tiles = d_s5 // (2 * LANES)
    per_tile = slabs // tiles
    ys = []
    for n in range(tiles):
        acc = None
        for s in range(n * per_tile, (n + 1) * per_tile):
            part = _dot(load(s), c_ref[s])
            acc = part if acc is None else acc + part
        ys.append(acc)
    return jnp.concatenate(ys, axis=-1)


def _s5_output(y, u, d_ref, wglu_ref):
    y = y + d_ref[...] * u
    y = 0.5 * y * (1.0 + lax.erf(y * math.sqrt(0.5)))
    return y * jax.nn.sigmoid(_dot(y.astype(BF16), wglu_ref[...]))


def _s5_prompt_kernel(x_ref, wu_ref, bre_ref, bim_ref, c_ref, are_ref, aim_ref, d_ref, wglu_ref,
                      ob_ref, hre_ref, him_ref, xr_ref, xi_ref, u_ref, y_ref, *, tile, sub, bsz):
    i = pl.program_id(0)
    pitch = sub + S5_PAD
    subs = tile // sub
    slabs = xr_ref.shape[1]
    d_s5 = d_ref.shape[-1]

    @pl.when(i == 0)
    def _():
        hre_ref[...] = jnp.zeros_like(hre_ref)
        him_ref[...] = jnp.zeros_like(him_ref)
        xr_ref[...] = jnp.zeros_like(xr_ref)
        xi_ref[...] = jnp.zeros_like(xi_ref)

    def drive(k):
        x = x_ref[:, k * sub:(k + 1) * sub, :].reshape(bsz * sub, x_ref.shape[-1])
        u = _dot(x.astype(BF16), wu_ref[...])
        u_ref[k] = u

        def store(slab, re, im):
            for b in range(bsz):
                xr_ref[k, slab, b * pitch:b * pitch + sub, :] = re[b * sub:(b + 1) * sub, :]
                xi_ref[k, slab, b * pitch:b * pitch + sub, :] = im[b * sub:(b + 1) * sub, :]

        _s5_input_drive(u.astype(BF16), bre_ref, bim_ref, store)

    def scan(k, hr, hi):
        for tt in range(sub):
            rows = pl.ds(tt, bsz, stride=pitch)
            for s in range(slabs):
                ar = are_ref[:, s * LANES:(s + 1) * LANES]
                ai = aim_ref[:, s * LANES:(s + 1) * LANES]
                nr = ar * hr[s] - ai * hi[s] + xr_ref[k, s, rows, :]
                ni = ar * hi[s] + ai * hr[s] + xi_ref[k, s, rows, :]
                xr_ref[k, s, rows, :] = nr
                xi_ref[k, s, rows, :] = ni
                hr[s], hi[s] = nr, ni
        return hr, hi

    def readout(k):
        load = lambda s: jnp.concatenate([xr_ref[k, s], xi_ref[k, s]], axis=-1).astype(BF16)
        y_ref[k] = _s5_readout(load, c_ref, d_s5)
        y = jnp.concatenate([y_ref[k, b * pitch:b * pitch + sub, :] for b in range(bsz)], axis=0)
        o = _s5_output(y, u_ref[k], d_ref, wglu_ref)
        ob_ref[:, k * sub:(k + 1) * sub, :] = o.reshape(bsz, sub, d_s5).astype(ob_ref.dtype)

    for k in range(subs):
        drive(k)
    hr = [hre_ref[:, s * LANES:(s + 1) * LANES] for s in range(slabs)]
    hi = [him_ref[:, s * LANES:(s + 1) * LANES] for s in range(slabs)]
    for k in range(subs):
        hr, hi = scan(k, hr, hi)
        readout(k)
    for s in range(slabs):
        hre_ref[:, s * LANES:(s + 1) * LANES] = hr[s]
        him_ref[:, s * LANES:(s + 1) * LANES] = hi[s]


def _layer_block(a, layer, grid_rank):
    idx = (layer,) + (0,) * (a.ndim - 1)
    index_map = {1: lambda i: idx, 2: lambda b, t: idx}[grid_rank]
    return pl.BlockSpec((None,) + a.shape[1:], index_map)


def _s5_prompt(x, w_in, layer, s5p):
    bsz, seq, d_model = x.shape
    d_s5 = s5p[5].shape[-1]
    n_state = s5p[3].shape[-1]
    slabs = n_state // LANES
    tile = min(S5_TILE, seq)
    sub = min(S5_SUB, tile)
    assert seq % tile == 0 and tile % sub == 0 and bsz == SUBLANES and w_in.shape[-1] % d_s5 == 0
    subs = tile // sub
    pitch = sub + S5_PAD
    u_block = w_in.shape[-1] // d_s5 - 1
    const = lambda *shape: pl.BlockSpec(shape, lambda i: (0,) * len(shape))
    return pl.pallas_call(
        functools.partial(_s5_prompt_kernel, tile=tile, sub=sub, bsz=bsz),
        grid=(seq // tile,),
        in_specs=[
            pl.BlockSpec((bsz, tile, d_model), lambda i: (0, i, 0)),
            pl.BlockSpec((None, d_model, d_s5), lambda i: (layer, 0, u_block)),
        ] + [_layer_block(a, layer, 1) for a in s5p],
        out_specs=[
            pl.BlockSpec((bsz, tile, d_s5), lambda i: (0, i, 0)),
            const(bsz, n_state), const(bsz, n_state),
        ],
        out_shape=[
            jax.ShapeDtypeStruct((bsz, seq, d_s5), BF16),
            jax.ShapeDtypeStruct((bsz, n_state), F32),
            jax.ShapeDtypeStruct((bsz, n_state), F32),
        ],
        scratch_shapes=[
            pltpu.VMEM((subs, slabs, bsz * pitch, LANES), F32),
            pltpu.VMEM((subs, slabs, bsz * pitch, LANES), F32),
            pltpu.VMEM((subs, bsz * sub, d_s5), F32),
            pltpu.VMEM((subs, bsz * pitch, d_s5), F32),
        ],
        compiler_params=pltpu.CompilerParams(
            dimension_semantics=("arbitrary",), vmem_limit_bytes=VMEM_LIMIT),
        name="s5_prompt",
    )(x, w_in, *s5p)


def _mix_sample_kernel(x_ref, w_ref, lb_ref, ng_ref, s_ref, hre_ref, him_ref,
                       bre_ref, bim_ref, c_ref, are_ref, aim_ref, d_ref, wglu_ref, so_all_ref,
                       oa_ref, ob_ref, so_ref, nre_ref, nim_ref, proj_ref, orow_ref, *, heads):
    del so_all_ref
    i = pl.program_id(0)
    d_h = heads * HEAD_DIM
    d_s5 = d_ref.shape[-1]
    slabs = are_ref.shape[-1] // LANES

    @pl.when(i == 0)
    def _():
        proj = _dot(x_ref[...].astype(BF16), w_ref[...])
        proj_ref[...] = proj
        u = proj[:, 4 * d_h:]

        def store(slab, re, im):
            sl = slice(slab * LANES, (slab + 1) * LANES)
            ar, ai = are_ref[:, sl], aim_ref[:, sl]
            hr, hi = hre_ref[:, sl], him_ref[:, sl]
            nre_ref[:, sl] = ar * hr - ai * hi + re
            nim_ref[:, sl] = ar * hi + ai * hr + im

        _s5_input_drive(u.astype(BF16), bre_ref, bim_ref, store)
        load = lambda s: jnp.concatenate(
            [nre_ref[:, s * LANES:(s + 1) * LANES], nim_ref[:, s * LANES:(s + 1) * LANES]],
            axis=-1).astype(BF16)
        y = _s5_readout(load, c_ref, d_s5)
        ob_ref[...] = _s5_output(y, u, d_ref, wglu_ref).astype(ob_ref.dtype)

    r0 = pl.multiple_of(i * SAMPLE_ROWS, SAMPLE_ROWS)
    rows = proj_ref[pl.ds(r0, SAMPLE_ROWS), :]
    q = rows[:, 0:d_h]
    fz = rows[:, d_h:2 * d_h]
    v = rows[:, 2 * d_h:3 * d_h]
    g = rows[:, 3 * d_h:4 * d_h]
    log_f, k = _hgrn_gates(fz, lb_ref[...])
    f = jnp.exp(log_f)
    per = heads * SAMPLE_ROWS
    pieces = [a[:, h * HEAD_DIM:(h + 1) * HEAD_DIM] for a in (q, f, k) for h in range(heads)]
    assert 3 * per <= HEAD_DIM
    pieces.append(jnp.zeros((HEAD_DIM - 3 * per, HEAD_DIM), F32))
    cols = jnp.concatenate(pieces, axis=0).T
    for h in range(heads):
        for r in range(SAMPLE_ROWS):
            j = h * SAMPLE_ROWS + r
            qc = cols[:, j:j + 1]
            fc = cols[:, per + j:per + j + 1]
            kc = cols[:, 2 * per + j:2 * per + j + 1]
            sn = fc * s_ref[r, h] + kc * v[r:r + 1, h * HEAD_DIM:(h + 1) * HEAD_DIM]
            so_ref[r, h] = sn
            orow_ref[r:r + 1, h * HEAD_DIM:(h + 1) * HEAD_DIM] = jnp.sum(qc * sn, axis=0, keepdims=True)
    oa_ref[pl.ds(r0, SAMPLE_ROWS), :] = _head_rms_gate(orow_ref[...], g, ng_ref[...]).astype(oa_ref.dtype)


def _mix_sample(x, w_in, layer, lb, ng, s_all, h_re, h_im, s5p, s_new_all):
    n, d_model = x.shape
    d_h = lb.shape[-1]
    heads = d_h // HEAD_DIM
    d_s5 = s5p[5].shape[-1]
    n_state = s5p[3].shape[-1]
    assert n % SAMPLE_ROWS == 0
    const = lambda *shape: pl.BlockSpec(shape, lambda i: (0,) * len(shape))
    state_spec = pl.BlockSpec((None, SAMPLE_ROWS, heads, HEAD_DIM, HEAD_DIM), lambda i: (layer, i, 0, 0, 0))
    args = [x, w_in, lb, ng, s_all, h_re, h_im, *s5p, s_new_all]
    in_specs = [const(n, d_model), _layer_block(w_in, layer, 1), const(1, d_h), const(1, d_h),
                state_spec, _layer_block(h_re, layer, 1), _layer_block(h_im, layer, 1)]
    in_specs += [_layer_block(a, layer, 1) for a in s5p]
    in_specs.append(pl.BlockSpec(memory_space=pl.ANY))
    aliases = {len(args) - 1: 2}
    return pl.pallas_call(
        functools.partial(_mix_sample_kernel, heads=heads),
        grid=(n // SAMPLE_ROWS,),
        in_specs=in_specs,
        out_specs=[
            const(n, d_h), const(n, d_s5), state_spec, const(n, n_state), const(n, n_state),
        ],
        out_shape=[
            jax.ShapeDtypeStruct((n, d_h), BF16),
            jax.ShapeDtypeStruct((n, d_s5), BF16),
            jax.ShapeDtypeStruct(s_all.shape, F32),
            jax.ShapeDtypeStruct((n, n_state), F32),
            jax.ShapeDtypeStruct((n, n_state), F32),
        ],
        scratch_shapes=[
            pltpu.VMEM((n, w_in.shape[-1]), F32),
            pltpu.VMEM((SAMPLE_ROWS, d_h), F32),
        ],
        input_output_aliases=aliases,
        compiler_params=pltpu.CompilerParams(
            dimension_semantics=("arbitrary",), vmem_limit_bytes=VMEM_LIMIT),
        name="mix_sample",
    )(*args)


def _mix_ln1(x, oa, ob, wout_ref, g1_ref, b1_ref, alpha):
    d_a = oa.shape[-1]
    mix = _dot(oa, wout_ref[0:d_a, :]) + _dot(ob, wout_ref[d_a:, :])
    return _layer_norm(alpha * x + mix, g1_ref[...], b1_ref[...])


def _ffn_blocks(x1b, wup_ref, cw_ref, cb_ref, wdown_ref, taps, d_ff):
    acc = None
    for j in range(d_ff // FFN_COLS):
        hs = []
        for base in (0, d_ff):
            cols = slice(base + j * FFN_COLS, base + (j + 1) * FFN_COLS)
            up = _dot(x1b, wup_ref[:, cols])
            m2, m1 = taps(up, cols)
            hs.append(cb_ref[:, cols] + m2 * cw_ref[0:1, cols] + m1 * cw_ref[1:2, cols]
                      + up * cw_ref[2:3, cols])
        val, gate = hs
        hh = (gate * jax.nn.sigmoid(gate)) * val
        part = _dot(hh.astype(BF16), wdown_ref[j * FFN_COLS:(j + 1) * FFN_COLS, :])
        acc = part if acc is None else acc + part
    return acc


def _ffn_prompt_kernel(x_ref, oa_ref, ob_ref, wout_ref, g1_ref, b1_ref, wup_ref, cw_ref, cb_ref,
                       wdown_ref, g2_ref, b2_ref, y_ref, cache_ref,
                       carry_ref, up_ref, hh_ref, acc_ref, x1_ref, x1b_ref, *, tile, alpha, d_ff):
    t = pl.program_id(1)
    lead = SUBLANES
    nb = d_ff // FFN_COLS
    sub = x1_ref.shape[1]
    subs = tile // sub
    slabs = FFN_COLS // LANES
    half_rows = FFN_ROWS // 2

    @pl.when(t == 0)
    def _():
        carry_ref[...] = jnp.zeros_like(carry_ref)

    def mix_ln1(s):
        rows = slice(s * sub, (s + 1) * sub)
        d_a = oa_ref.shape[-1]
        acc_ref[s] = _dot(oa_ref[rows, :], wout_ref[0:d_a, :]) + _dot(ob_ref[rows, :], wout_ref[d_a:, :])
        for r in range(0, sub, FFN_ROWS):
            rr = slice(r, r + FFN_ROWS)
            x1 = _layer_norm(alpha * x_ref[s * sub + r:s * sub + r + FFN_ROWS, :] + acc_ref[s, rr, :],
                             g1_ref[...], b1_ref[...])
            x1_ref[s, rr, :] = x1
            x1b_ref[s, rr, :] = x1.astype(BF16)

    def ln2(s):
        for r in range(0, sub, FFN_ROWS):
            rr = slice(r, r + FFN_ROWS)
            y_ref[s * sub + r:s * sub + r + FFN_ROWS, :] = _layer_norm(
                alpha * x1_ref[s, rr, :] + acc_ref[s, rr, :], g2_ref[...], b2_ref[...])

    def halves(j):
        return (slice(j * FFN_COLS, (j + 1) * FFN_COLS),
                slice(d_ff + j * FFN_COLS, d_ff + (j + 1) * FFN_COLS))

    def up_block(slot, s, j):
        for half, cols in enumerate(halves(j)):
            up = _dot(x1b_ref[s], wup_ref[:, cols])
            for i in range(slabs):
                lanes = slice(cols.start + i * LANES, cols.start + (i + 1) * LANES)
                up_ref[slot, half * slabs + i, lead:lead + sub, :] = up[:, i * LANES:(i + 1) * LANES]
                up_ref[slot, half * slabs + i, lead - 2:lead, :] = carry_ref[:, lanes]

    def conv_block(slot, s, j):
        for i in range(slabs):
            lanes = [slice(c.start + i * LANES, c.start + (i + 1) * LANES) for c in halves(j)]
            rep = lambda a: jnp.broadcast_to(a, (half_rows, LANES))
            prm = [(rep(cw_ref[0:1, l]), rep(cw_ref[1:2, l]), rep(cw_ref[2:3, l]), rep(cb_ref[:, l]))
                   for l in lanes]
            for r in range(0, sub, FFN_ROWS):
                res = []
                for half, (w0, w1, w2, cb) in enumerate(prm):
                    tap = lambda d: up_ref[slot, half * slabs + i, pl.ds(lead + r + d, half_rows, stride=2), :]
                    em, om, e, o = tap(-2), tap(-1), tap(0), tap(1)
                    res.append((cb + em * w0 + om * w1 + e * w2, cb + om * w0 + e * w1 + o * w2))
                (ve, vo), (ge, go) = res
                hh_ref[slot, i, pl.ds(r, half_rows, stride=2), :] = (ge * jax.nn.sigmoid(ge)) * ve
                hh_ref[slot, i, pl.ds(r + 1, half_rows, stride=2), :] = (go * jax.nn.sigmoid(go)) * vo
            for half, l in enumerate(lanes):
                carry_ref[:, l] = up_ref[slot, half * slabs + i, lead + sub - 2:lead + sub, :]

    def down_block(slot, s, j):
        hh = jnp.concatenate([hh_ref[slot, i] for i in range(slabs)], axis=-1).astype(BF16)
        part = _dot(hh, wdown_ref[j * FFN_COLS:(j + 1) * FFN_COLS, :])
        if j == 0:
            acc_ref[s] = part
        else:
            acc_ref[s] += part

    for s in range(subs):
        mix_ln1(s)
    blocks = [(s, j) for s in range(subs) for j in range(nb)]
    up_block(0, *blocks[0])
    for n, (s, j) in enumerate(blocks):
        if n + 1 < len(blocks):
            up_block((n + 1) % 2, *blocks[n + 1])
        conv_block(n % 2, s, j)
        down_block(n % 2, s, j)
        if j == nb - 1:
            ln2(s)
    cache_ref[...] = carry_ref[...]


def _ffn_sample_kernel(x_ref, oa_ref, ob_ref, m2_ref, m1_ref, wout_ref, g1_ref, b1_ref, wup_ref, cw_ref,
                       cb_ref, wdown_ref, g2_ref, b2_ref, up_all_ref, y_ref, up_ref, *, alpha, d_ff):
    del up_all_ref
    x1 = _mix_ln1(x_ref[...], oa_ref[...], ob_ref[...], wout_ref, g1_ref, b1_ref, alpha)

    def taps(up, cols):
        up_ref[:, cols] = up
        return m2_ref[:, cols], m1_ref[:, cols]

    ff = _ffn_blocks(x1.astype(BF16), wup_ref, cw_ref, cb_ref, wdown_ref, taps, d_ff)
    y_ref[...] = _layer_norm(alpha * x1 + ff, g2_ref[...], b2_ref[...])


def _resident(a, layer, grid_rank):
    idx = (layer,) + (0,) * (a.ndim - 1)
    index_map = {1: lambda i: idx, 2: lambda b, t: idx}[grid_rank]
    return pl.BlockSpec((None,) + a.shape[1:], index_map, pipeline_mode=pl.Buffered(1))


def _ffn_prompt(x, oa, ob, fw, layer, alpha):
    bsz, seq, d_model = x.shape
    d_ff = fw[6].shape[1]
    tile = min(FFN_TILE, seq)
    sub = min(FFN_SUB, tile)
    assert seq % tile == 0 and d_ff % FFN_COLS == 0 and tile % sub == 0 and sub % FFN_ROWS == 0
    subs = tile // sub
    tok = lambda d: pl.BlockSpec((None, tile, d), lambda b, t: (b, t, 0))
    return pl.pallas_call(
        functools.partial(_ffn_prompt_kernel, tile=tile, alpha=alpha, d_ff=d_ff),
        grid=(bsz, seq // tile),
        in_specs=[tok(d_model), tok(oa.shape[-1]), tok(ob.shape[-1])] + [_resident(a, layer, 2) for a in fw],
        out_specs=[tok(d_model), pl.BlockSpec((None, 2, 2 * d_ff), lambda b, t: (b, 0, 0))],
        out_shape=[jax.ShapeDtypeStruct(x.shape, F32),
                   jax.ShapeDtypeStruct((bsz, 2, 2 * d_ff), F32)],
        scratch_shapes=[
            pltpu.VMEM((2, 2 * d_ff), F32),
            pltpu.VMEM((2, 2 * FFN_COLS // LANES, SUBLANES + sub, LANES), F32),
            pltpu.VMEM((2, FFN_COLS // LANES, sub, LANES), F32),
            pltpu.VMEM((subs, sub, d_model), F32),
            pltpu.VMEM((subs, sub, d_model), F32),
            pltpu.VMEM((subs, sub, d_model), BF16),
        ],
        compiler_params=pltpu.CompilerParams(
            dimension_semantics=("arbitrary", "arbitrary"), vmem_limit_bytes=VMEM_LIMIT),
        name="ffn_prompt",
    )(x, oa, ob, *fw)


def _ffn_sample(x, oa, ob, tap2_all, tap1_all, fw, layer, alpha, up_all):
    n, d_model = x.shape
    d_ff = fw[6].shape[1]
    whole = lambda a: pl.BlockSpec(a.shape, lambda i: (0,) * a.ndim, pipeline_mode=pl.Buffered(1))
    args = [x, oa, ob, tap2_all, tap1_all, *fw, up_all]
    in_specs = [whole(x), whole(oa), whole(ob), _resident(tap2_all, layer, 1), _resident(tap1_all, layer, 1)]
    in_specs += [_resident(a, layer, 1) for a in fw]
    in_specs.append(pl.BlockSpec(memory_space=pl.ANY))
    aliases = {len(args) - 1: 1}
    return pl.pallas_call(
        functools.partial(_ffn_sample_kernel, alpha=alpha, d_ff=d_ff),
        grid=(1,),
        in_specs=in_specs,
        out_specs=[pl.BlockSpec(x.shape, lambda i: (0, 0)),
                   pl.BlockSpec((None,) + up_all.shape[1:], lambda i: (layer, 0, 0))],
        out_shape=[jax.ShapeDtypeStruct(x.shape, F32), jax.ShapeDtypeStruct(up_all.shape, F32)],
        input_output_aliases=aliases,
        compiler_params=pltpu.CompilerParams(
            dimension_semantics=("arbitrary",), vmem_limit_bytes=VMEM_LIMIT),
        name="ffn_sample",
    )(*args)


def _s5_params(lam_re, lam_im, log_dt, b_re, b_im, c_re, c_im, d_skip, w_glu):
    groups, n_p = lam_re.shape
    lr = jnp.minimum(lam_re, -1e-4)
    li = lam_im
    dt = jnp.exp(log_dt)[:, None]
    mag = jnp.exp(lr * dt)
    ab_re = mag * jnp.cos(li * dt)
    ab_im = mag * jnp.sin(li * dt)
    den = lr * lr + li * li
    nr = ab_re - 1.0
    coef_re = (nr * lr + ab_im * li) / den
    coef_im = (ab_im * lr - nr * li) / den
    bb_re = coef_re[..., None] * b_re - coef_im[..., None] * b_im
    bb_im = coef_re[..., None] * b_im + coef_im[..., None] * b_re

    gpb = LANES // S5_GROUP_CH
    kblocks = groups // gpb
    eye = jnp.eye(gpb, dtype=F32)

    def b_blocks(bb):
        bb = bb.reshape(kblocks, gpb, n_p, S5_GROUP_CH)
        return jnp.einsum('mgpj,gh->mgjhp', bb, eye).reshape(kblocks, LANES, gpb * n_p).astype(BF16)

    gps = LANES // n_p
    slabs = groups // gps
    gpt = (2 * LANES) // S5_GROUP_CH
    pos = (jnp.arange(slabs)[:, None] * gps + jnp.arange(gps)[None, :]) % gpt
    onehot = jax.nn.one_hot(pos, gpt, dtype=F32)

    def c_half(c):
        c = c.reshape(slabs, gps, S5_GROUP_CH, n_p)
        return jnp.einsum('sgjp,sgh->sgphj', c, onehot).reshape(slabs, gps * n_p, gpt * S5_GROUP_CH)

    cblk = jnp.concatenate([c_half(c_re), -c_half(c_im)], axis=1).astype(BF16)
    return (b_blocks(bb_re), b_blocks(bb_im), cblk,
            ab_re.reshape(1, groups * n_p), ab_im.reshape(1, groups * n_p),
            d_skip.reshape(1, -1), w_glu.astype(BF16))


def kernel(x_prompt, x_sample, state_hgrn, state_s5_re, state_s5_im, cache_ffn_conv, w_in, hgrn_lb_logits, hgrn_norm_g, s5_lambda_re, s5_lambda_im, s5_log_dt, s5_b_re, s5_b_im, s5_c_re, s5_c_im, s5_d, w_glu, w_out, ln1_g, ln1_b, w_ffn_up, ffn_conv_w, ffn_conv_b, w_ffn_down, ln2_g, ln2_b):
    depth = w_in.shape[0]
    d_h = hgrn_lb_logits.shape[-1]
    heads = d_h // HEAD_DIM
    n_dec = x_sample.shape[0]
    groups, n_p = s5_lambda_re.shape[1:]
    alpha = (2 * depth) ** 0.25

    sm = jax.nn.softmax(hgrn_lb_logits.astype(F32), axis=0)
    lower_bounds = jnp.cumsum(sm, axis=0) - sm[0:1]

    w_in_b = w_in.astype(BF16)
    s5p = jax.vmap(_s5_params)(s5_lambda_re, s5_lambda_im, s5_log_dt, s5_b_re, s5_b_im,
                               s5_c_re, s5_c_im, s5_d, w_glu)
    rows = lambda a: a.reshape(depth, 1, -1)
    fw = (w_out.astype(BF16), rows(ln1_g), rows(ln1_b), w_ffn_up.astype(BF16),
          ffn_conv_w, rows(ffn_conv_b), w_ffn_down.astype(BF16), rows(ln2_g), rows(ln2_b))
    h_re = state_s5_re.reshape(depth, n_dec, groups * n_p)
    h_im = state_s5_im.reshape(depth, n_dec, groups * n_p)
    tap2_all = cache_ffn_conv[:, :, 0, :]
    tap1_all = cache_ffn_conv[:, :, 1, :]

    xp = x_prompt
    xs = x_sample.reshape(n_dec, -1)
    outs = {k: [] for k in ("hp", "rp", "ip", "cp", "rs", "is")}
    hs_new = jnp.zeros(state_hgrn.shape, F32)
    up_new = jnp.zeros(tap1_all.shape, F32)
    for l in range(depth):
        lb = lower_bounds[l].reshape(1, d_h)
        ng = jnp.tile(hgrn_norm_g[l], heads).reshape(1, d_h)

        oa, s_new = _hgrn_prompt(xp, w_in_b, l, lb, ng)
        ob, re_new, im_new = _s5_prompt(xp, w_in_b, l, s5p)
        xp, cache_new = _ffn_prompt(xp, oa, ob, fw, l, alpha)
        outs["hp"].append(s_new)
        outs["rp"].append(re_new.reshape(-1, groups, n_p))
        outs["ip"].append(im_new.reshape(-1, groups, n_p))
        outs["cp"].append(cache_new)

        oa, ob, hs_new, re_new, im_new = _mix_sample(xs, w_in_b, l, lb, ng, state_hgrn, h_re, h_im, s5p, hs_new)
        xs, up_new = _ffn_sample(xs, oa, ob, tap2_all, tap1_all, fw, l, alpha, up_new)
        outs["rs"].append(re_new.reshape(n_dec, groups, n_p))
        outs["is"].append(im_new.reshape(n_dec, groups, n_p))

    st = {k: jnp.stack(v) for k, v in outs.items()}
    cs_new = jnp.stack([tap1_all, up_new], axis=2)
    return (xp, xs.reshape(x_sample.shape), st["hp"], st["rp"], st["ip"], st["cp"],
            hs_new, st["rs"], st["is"], cs_new)
```

```python
import functools
import math

import jax
import jax.numpy as jnp
import numpy as np
from jax import lax
from jax.experimental import pallas as pl
from jax.experimental.pallas import tpu as pltpu

F32 = jnp.float32
BF16 = jnp.bfloat16

HEAD_DIM = 128
CHUNK = 64
S5_GROUP_CH = 16
S5_STATE = 64
LANES = 128
SUBLANES = 8
LN_EPS = 1e-5
RMS_EPS = 1e-6
F_FLOOR = 1e-20
SAFE_SPAN = 60.0
VMEM_LIMIT = 56 * 1024 * 1024

HGRN_TILE = 512
HGRN_SUB = 256
S5_TILE = 128
S5_SUB = 32
S5_PAD = 4
FFN_TILE = 512
FFN_SUB = 256
FFN_COLS = 256
FFN_ROWS = 32
SAMPLE_ROWS = 8


def _dot(a, b):
    return jnp.dot(a, b, preferred_element_type=F32)


def _dot_nt(a, b):
    return lax.dot_general(a, b, (((1,), (1,)), ((), ())), preferred_element_type=F32)


def _dot_tn(a, b):
    return lax.dot_general(a, b, (((0,), (0,)), ((), ())), preferred_element_type=F32)


def _split3(x):
    hi = x.astype(BF16)
    r1 = x - hi.astype(F32)
    mid = r1.astype(BF16)
    lo = (r1 - mid.astype(F32)).astype(BF16)
    return hi, mid, lo


def _dot3(m, parts):
    return _dot(m, parts[0]) + _dot(m, parts[1]) + _dot(m, parts[2])


def _cumsum_rows(x):
    n, w = x.shape
    tiles = n // SUBLANES
    x3 = x.reshape(tiles, SUBLANES, w)
    sub = lax.broadcasted_iota(jnp.int32, (1, SUBLANES, 1), 1)
    s = 1
    while s < SUBLANES:
        x3 = x3 + jnp.where(sub >= s, pltpu.roll(x3, s, axis=1), 0.0)
        s *= 2
    outs = [x3[0]]
    for i in range(1, tiles):
        outs.append(x3[i] + outs[-1][SUBLANES - 1:SUBLANES, :])
    return jnp.concatenate(outs, axis=0)


def _layer_norm(x, g, b):
    mu = jnp.mean(x, axis=-1, keepdims=True)
    xc = x - mu
    var = jnp.mean(xc * xc, axis=-1, keepdims=True)
    return xc * lax.rsqrt(var + LN_EPS) * g + b


def _hgrn_gates(fz, lb):
    oml = 1.0 - lb
    f = lb + oml * jax.nn.sigmoid(fz)
    log_f = jnp.log(jnp.maximum(f, F_FLOOR))
    k = oml * jax.nn.sigmoid(-fz)
    return log_f, k


def _head_rms_gate(o, g, ng):
    heads = o.shape[-1] // HEAD_DIM
    outs = []
    for h in range(heads):
        oh = o[:, h * HEAD_DIM:(h + 1) * HEAD_DIM]
        ms = jnp.mean(oh * oh, axis=-1, keepdims=True)
        outs.append(oh * lax.rsqrt(ms + RMS_EPS))
    return jnp.concatenate(outs, axis=-1) * ng * jax.nn.sigmoid(g)


def _hgrn_prompt_kernel(x_ref, w_ref, lb_ref, ng_ref, lvl_ref, oa_ref, s_ref,
                        st_ref, st0_ref, proj_ref, *, tile, heads):
    t = pl.program_id(1)
    d_h = heads * HEAD_DIM

    @pl.when(t == 0)
    def _():
        st_ref[...] = jnp.zeros_like(st_ref)

    sub = min(HGRN_SUB, tile)

    pieces = sub // CHUNK
    piece_cols = 4 * d_h // pieces

    def project(k, j):
        rows = slice(k * sub, (k + 1) * sub)
        cols = slice(j * piece_cols, (j + 1) * piece_cols)
        proj_ref[rows, cols] = _dot(x_ref[rows, :].astype(BF16), w_ref[:, cols])

    st0_ref[...] = st_ref[...]
    lb = lb_ref[...]
    ng = ng_ref[...]
    ri = lax.broadcasted_iota(jnp.int32, (CHUNK, CHUNK), 0)
    ci = lax.broadcasted_iota(jnp.int32, (CHUNK, CHUNK), 1)
    row = lax.broadcasted_iota(jnp.int32, (CHUNK, 1), 0)
    mid = CHUNK // 2 - 1

    def anchored_scores(q, k, b, log_f):
        bm = b[mid:mid + 1, :]
        qe = (q * jnp.exp(jnp.minimum(b - bm, SAFE_SPAN))).astype(BF16)
        ke = (k * jnp.exp(jnp.minimum(bm - b, SAFE_SPAN))).astype(BF16)
        return [jnp.where(ri >= ci, _dot_nt(qe[:, h * HEAD_DIM:(h + 1) * HEAD_DIM],
                                             ke[:, h * HEAD_DIM:(h + 1) * HEAD_DIM]), 0.0)
                for h in range(heads)]

    def hierarchical_scores(q, k, b, log_f):
        anchors = _dot3(lvl_ref[...], _split3(log_f))
        qb = q.astype(BF16)
        kb = k.astype(BF16)
        sc = [jnp.where(ri == ci, _dot_nt(qb[:, h * HEAD_DIM:(h + 1) * HEAD_DIM],
                                          kb[:, h * HEAD_DIM:(h + 1) * HEAD_DIM]), 0.0)
              for h in range(heads)]
        lvl = 0
        hs = CHUNK // 2
        while hs >= 1:
            a = anchors[lvl * CHUNK:(lvl + 1) * CHUNK, :]
            upper = (row & (2 * hs - 1)) >= hs
            qe = (q * jnp.where(upper, jnp.exp(jnp.minimum(b - a, 0.0)), 0.0)).astype(BF16)
            ke = (k * jnp.where(upper, 0.0, jnp.exp(jnp.minimum(a - b, 0.0)))).astype(BF16)
            shift = int(math.log2(2 * hs))
            same = (ri >> shift) == (ci >> shift)
            for h in range(heads):
                sl = slice(h * HEAD_DIM, (h + 1) * HEAD_DIM)
                sc[h] = sc[h] + jnp.where(same, _dot_nt(qe[:, sl], ke[:, sl]), 0.0)
            lvl += 1
            hs //= 2
        return sc

    def chunk(rows, scores_fn):
        q = proj_ref[rows, 0:d_h]
        fz = proj_ref[rows, d_h:2 * d_h]
        v = proj_ref[rows, 2 * d_h:3 * d_h]
        g = proj_ref[rows, 3 * d_h:4 * d_h]
        log_f, k = _hgrn_gates(fz, lb)
        b = _cumsum_rows(log_f)
        bm = b[mid:mid + 1, :]
        bl = b[CHUNK - 1:CHUNK, :]
        sc = scores_fn(q, k, b, log_f)
        qs = (q * jnp.exp(b)).astype(BF16)
        kl = (k * jnp.exp(bl - b)).astype(BF16)
        vb = v.astype(BF16)
        dec = jnp.exp(bl)
        outs = []
        for h in range(heads):
            sl = slice(h * HEAD_DIM, (h + 1) * HEAD_DIM)
            st = st_ref[h]
            outs.append(_dot(sc[h].astype(BF16), vb[:, sl]) + _dot_nt(qs[:, sl], st.astype(BF16)))
            st_ref[h] = st * dec[:, sl] + _dot_tn(vb[:, sl], kl[:, sl])
        o = jnp.concatenate(outs, axis=-1)
        oa_ref[rows, :] = _head_rms_gate(o, g, ng).astype(oa_ref.dtype)
        return jnp.minimum(bm, bl - bm)

    span = None
    for j in range(pieces):
        project(0, j)
    for c in range(tile // CHUNK):
        k, j = divmod(c, pieces)
        if (k + 1) * sub < tile:
            project(k + 1, j)
        worst = chunk(slice(c * CHUNK, (c + 1) * CHUNK), anchored_scores)
        span = worst if span is None else jnp.minimum(span, worst)
    span_ok = jnp.min(span) >= -SAFE_SPAN

    @pl.when(jnp.logical_not(span_ok))
    def _():
        st_ref[...] = st0_ref[...]

        def redo(c, carry):
            chunk(pl.ds(pl.multiple_of(c * CHUNK, CHUNK), CHUNK), hierarchical_scores)
            return carry

        lax.fori_loop(0, tile // CHUNK, redo, 0)

    @pl.when(t == pl.num_programs(1) - 1)
    def _():
        for h in range(heads):
            s_ref[h] = st_ref[h].T


def _anchor_matrices():
    r = np.arange(CHUNK)
    lvls = []
    hs = CHUNK // 2
    while hs >= 1:
        anchor = (r // (2 * hs)) * (2 * hs) + hs - 1
        lvls.append((r[None, :] <= anchor[:, None]).astype(np.float32))
        hs //= 2
    return jnp.asarray(np.concatenate(lvls, axis=0), BF16)


def _hgrn_prompt(x, w_in, layer, lb, ng):
    bsz, seq, d_model = x.shape
    d_h = lb.shape[-1]
    heads = d_h // HEAD_DIM
    tile = min(HGRN_TILE, seq)
    assert seq % tile == 0 and tile % CHUNK == 0
    lvl = _anchor_matrices()
    const = lambda *shape: pl.BlockSpec(shape, lambda b, t: (0,) * len(shape))
    return pl.pallas_call(
        functools.partial(_hgrn_prompt_kernel, tile=tile, heads=heads),
        grid=(bsz, seq // tile),
        in_specs=[
            pl.BlockSpec((None, tile, d_model), lambda b, t: (b, t, 0)),
            pl.BlockSpec((None, d_model, 4 * d_h), lambda b, t: (layer, 0, 0)),
            const(1, d_h),
            const(1, d_h),
            const(lvl.shape[0], CHUNK),
        ],
        out_specs=[
            pl.BlockSpec((None, tile, d_h), lambda b, t: (b, t, 0)),
            pl.BlockSpec((None, heads, HEAD_DIM, HEAD_DIM), lambda b, t: (b, 0, 0, 0)),
        ],
        out_shape=[
            jax.ShapeDtypeStruct((bsz, seq, d_h), BF16),
            jax.ShapeDtypeStruct((bsz, heads, HEAD_DIM, HEAD_DIM), F32),
        ],
        scratch_shapes=[
            pltpu.VMEM((heads, HEAD_DIM, HEAD_DIM), F32),
            pltpu.VMEM((heads, HEAD_DIM, HEAD_DIM), F32),
            pltpu.VMEM((tile, 4 * d_h), F32),
        ],
        compiler_params=pltpu.CompilerParams(
            dimension_semantics=("arbitrary", "arbitrary"), vmem_limit_bytes=VMEM_LIMIT),
        name="hgrn_prompt",
    )(x, w_in, lb, ng, lvl)


def _s5_input_drive(ub, bre_ref, bim_ref, store):
    kblocks = bre_ref.shape[0]
    slabs_per_block = bre_ref.shape[2] // LANES
    for m in range(kblocks):
        um = ub[:, m * LANES:(m + 1) * LANES]
        re = _dot(um, bre_ref[m])
        im = _dot(um, bim_ref[m])
        for s in range(slabs_per_block):
            store(m * slabs_per_block + s, re[:, s * LANES:(s + 1) * LANES], im[:, s * LANES:(s + 1) * LANES])


def _s5_readout(load, c_ref, d_s5):
    slabs = c_ref.shape[0]
    tiles = d_s5 // (2 * LANES)
    per_tile = slabs // tiles
    ys = []
    for n in range(tiles):
        acc = None
        for s in range(n * per_tile, (n + 1) * per_tile):
            part = _dot(load(s), c_ref[s])
            acc = part if acc is None else acc + part
        ys.append(acc)
    return jnp.concatenate(ys, axis=-1)


def _s5_output(y, u, d_ref, wglu_ref):
    y = y + d_ref[...] * u
    y = 0.5 * y * (1.0 + lax.erf(y * math.sqrt(0.5)))
    return y * jax.nn.sigmoid(_dot(y.astype(BF16), wglu_ref[...]))


def _s5_prompt_kernel(x_ref, wu_ref, bre_ref, bim_ref, c_ref, are_ref, aim_ref, d_ref, wglu_ref,
                      ob_ref, hre_ref, him_ref, xr_ref, xi_ref, u_ref, y_ref, *, tile, sub, bsz):
    i = pl.program_id(0)
    pitch = sub + S5_PAD
    subs = tile // sub
    slabs = xr_ref.shape[1]
    d_s5 = d_ref.shape[-1]

    @pl.when(i == 0)
    def _():
        hre_ref[...] = jnp.zeros_like(hre_ref)
        him_ref[...] = jnp.zeros_like(him_ref)
        xr_ref[...] = jnp.zeros_like(xr_ref)
        xi_ref[...] = jnp.zeros_like(xi_ref)

    def drive(k):
        x = x_ref[:, k * sub:(k + 1) * sub, :].reshape(bsz * sub, x_ref.shape[-1])
        u = _dot(x.astype(BF16), wu_ref[...])
        u_ref[k] = u

        def store(slab, re, im):
            for b in range(bsz):
                xr_ref[k, slab, b * pitch:b * pitch + sub, :] = re[b * sub:(b + 1) * sub, :]
                xi_ref[k, slab, b * pitch:b * pitch + sub, :] = im[b * sub:(b + 1) * sub, :]

        _s5_input_drive(u.astype(BF16), bre_ref, bim_ref, store)

    def scan(k, hr, hi):
        for tt in range(sub):
            rows = pl.ds(tt, bsz, stride=pitch)
            for s in range(slabs):
                ar = are_ref[:, s * LANES:(s + 1) * LANES]
                ai = aim_ref[:, s * LANES:(s + 1) * LANES]
                nr = ar * hr[s] - ai * hi[s] + xr_ref[k, s, rows, :]
                ni = ar * hi[s] + ai * hr[s] + xi_ref[k, s, rows, :]
                xr_ref[k, s, rows, :] = nr
                xi_ref[k, s, rows, :] = ni
                hr[s], hi[s] = nr, ni
        return hr, hi

    def readout(k):
        load = lambda s: jnp.concatenate([xr_ref[k, s], xi_ref[k, s]], axis=-1).astype(BF16)
        y_ref[k] = _s5_readout(load, c_ref, d_s5)
        y = jnp.concatenate([y_ref[k, b * pitch:b * pitch + sub, :] for b in range(bsz)], axis=0)
        o = _s5_output(y, u_ref[k], d_ref, wglu_ref)
        ob_ref[:, k * sub:(k + 1) * sub, :] = o.reshape(bsz, sub, d_s5).astype(ob_ref.dtype)

    for k in range(subs):
        drive(k)
    hr = [hre_ref[:, s * LANES:(s + 1) * LANES] for s in range(slabs)]
    hi = [him_ref[:, s * LANES:(s + 1) * LANES] for s in range(slabs)]
    for k in range(subs):
        hr, hi = scan(k, hr, hi)
        readout(k)
    for s in range(slabs):
        hre_ref[:, s * LANES:(s + 1) * LANES] = hr[s]
        him_ref[:, s * LANES:(s + 1) * LANES] = hi[s]


def _layer_block(a, layer, grid_rank):
    idx = (layer,) + (0,) * (a.ndim - 1)
    index_map = {1: lambda i: idx, 2: lambda b, t: idx}[grid_rank]
    return pl.BlockSpec((None,) + a.shape[1:], index_map)


def _s5_prompt(x, w_in, layer, s5p):
    bsz, seq, d_model = x.shape
    d_s5 = s5p[5].shape[-1]
    n_state = s5p[3].shape[-1]
    slabs = n_state // LANES
    tile = min(S5_TILE, seq)
    sub = min(S5_SUB, tile)
    assert seq % tile == 0 and tile % sub == 0 and bsz == SUBLANES and w_in.shape[-1] % d_s5 == 0
    subs = tile // sub
    pitch = sub + S5_PAD
    u_block = w_in.shape[-1] // d_s5 - 1
    const = lambda *shape: pl.BlockSpec(shape, lambda i: (0,) * len(shape))
    return pl.pallas_call(
        functools.partial(_s5_prompt_kernel, tile=tile, sub=sub, bsz=bsz),
        grid=(seq // tile,),
        in_specs=[
            pl.BlockSpec((bsz, tile, d_model), lambda i: (0, i, 0)),
            pl.BlockSpec((None, d_model, d_s5), lambda i: (layer, 0, u_block)),
        ] + [_layer_block(a, layer, 1) for a in s5p],
        out_specs=[
            pl.BlockSpec((bsz, tile, d_s5), lambda i: (0, i, 0)),
            const(bsz, n_state), const(bsz, n_state),
        ],
        out_shape=[
            jax.ShapeDtypeStruct((bsz, seq, d_s5), BF16),
            jax.ShapeDtypeStruct((bsz, n_state), F32),
            jax.ShapeDtypeStruct((bsz, n_state), F32),
        ],
        scratch_shapes=[
            pltpu.VMEM((subs, slabs, bsz * pitch, LANES), F32),
            pltpu.VMEM((subs, slabs, bsz * pitch, LANES), F32),
            pltpu.VMEM((subs, bsz * sub, d_s5), F32),
            pltpu.VMEM((subs, bsz * pitch, d_s5), F32),
        ],
        compiler_params=pltpu.CompilerParams(
            dimension_semantics=("arbitrary",), vmem_limit_bytes=VMEM_LIMIT),
        name="s5_prompt",
    )(x, w_in, *s5p)


def _mix_sample_kernel(x_ref, w_ref, lb_ref, ng_ref, s_ref, hre_ref, him_ref,
                       bre_ref, bim_ref, c_ref, are_ref, aim_ref, d_ref, wglu_ref, so_all_ref,
                       oa_ref, ob_ref, so_ref, nre_ref, nim_ref, proj_ref, orow_ref, *, heads):
    del so_all_ref
    i = pl.program_id(0)
    d_h = heads * HEAD_DIM
    d_s5 = d_ref.shape[-1]
    slabs = are_ref.shape[-1] // LANES

    @pl.when(i == 0)
    def _():
        proj = _dot(x_ref[...].astype(BF16), w_ref[...])
        proj_ref[...] = proj
        u = proj[:, 4 * d_h:]

        def store(slab, re, im):
            sl = slice(slab * LANES, (slab + 1) * LANES)
            ar, ai = are_ref[:, sl], aim_ref[:, sl]
            hr, hi = hre_ref[:, sl], him_ref[:, sl]
            nre_ref[:, sl] = ar * hr - ai * hi + re
            nim_ref[:, sl] = ar * hi + ai * hr + im

        _s5_input_drive(u.astype(BF16), bre_ref, bim_ref, store)
        load = lambda s: jnp.concatenate(
            [nre_ref[:, s * LANES:(s + 1) * LANES], nim_ref[:, s * LANES:(s + 1) * LANES]],
            axis=-1).astype(BF16)
        y = _s5_readout(load, c_ref, d_s5)
        ob_ref[...] = _s5_output(y, u, d_ref, wglu_ref).astype(ob_ref.dtype)

    r0 = pl.multiple_of(i * SAMPLE_ROWS, SAMPLE_ROWS)
    rows = proj_ref[pl.ds(r0, SAMPLE_ROWS), :]
    q = rows[:, 0:d_h]
    fz = rows[:, d_h:2 * d_h]
    v = rows[:, 2 * d_h:3 * d_h]
    g = rows[:, 3 * d_h:4 * d_h]
    log_f, k = _hgrn_gates(fz, lb_ref[...])
    f = jnp.exp(log_f)
    per = heads * SAMPLE_ROWS
    pieces = [a[:, h * HEAD_DIM:(h + 1) * HEAD_DIM] for a in (q, f, k) for h in range(heads)]
    assert 3 * per <= HEAD_DIM
    pieces.append(jnp.zeros((HEAD_DIM - 3 * per, HEAD_DIM), F32))
    cols = jnp.concatenate(pieces, axis=0).T
    for h in range(heads):
        for r in range(SAMPLE_ROWS):
            j = h * SAMPLE_ROWS + r
            qc = cols[:, j:j + 1]
            fc = cols[:, per + j:per + j + 1]
            kc = cols[:, 2 * per + j:2 * per + j + 1]
            sn = fc * s_ref[r, h] + kc * v[r:r + 1, h * HEAD_DIM:(h + 1) * HEAD_DIM]
            so_ref[r, h] = sn
            orow_ref[r:r + 1, h * HEAD_DIM:(h + 1) * HEAD_DIM] = jnp.sum(qc * sn, axis=0, keepdims=True)
    oa_ref[pl.ds(r0, SAMPLE_ROWS), :] = _head_rms_gate(orow_ref[...], g, ng_ref[...]).astype(oa_ref.dtype)


def _mix_sample(x, w_in, layer, lb, ng, s_all, h_re, h_im, s5p, s_new_all):
    n, d_model = x.shape
    d_h = lb.shape[-1]
    heads = d_h // HEAD_DIM
    d_s5 = s5p[5].shape[-1]
    n_state = s5p[3].shape[-1]
    assert n % SAMPLE_ROWS == 0
    const = lambda *shape: pl.BlockSpec(shape, lambda i: (0,) * len(shape))
    state_spec = pl.BlockSpec((None, SAMPLE_ROWS, heads, HEAD_DIM, HEAD_DIM), lambda i: (layer, i, 0, 0, 0))
    args = [x, w_in, lb, ng, s_all, h_re, h_im, *s5p, s_new_all]
    in_specs = [const(n, d_model), _layer_block(w_in, layer, 1), const(1, d_h), const(1, d_h),
                state_spec, _layer_block(h_re, layer, 1), _layer_block(h_im, layer, 1)]
    in_specs += [_layer_block(a, layer, 1) for a in s5p]
    in_specs.append(pl.BlockSpec(memory_space=pl.ANY))
    aliases = {len(args) - 1: 2}
    return pl.pallas_call(
        functools.partial(_mix_sample_kernel, heads=heads),
        grid=(n // SAMPLE_ROWS,),
        in_specs=in_specs,
        out_specs=[
            const(n, d_h), const(n, d_s5), state_spec, const(n, n_state), const(n, n_state),
        ],
        out_shape=[
            jax.ShapeDtypeStruct((n, d_h), BF16),
            jax.ShapeDtypeStruct((n, d_s5), BF16),
            jax.ShapeDtypeStruct(s_all.shape, F32),
            jax.ShapeDtypeStruct((n, n_state), F32),
            jax.ShapeDtypeStruct((n, n_state), F32),
        ],
        scratch_shapes=[
            pltpu.VMEM((n, w_in.shape[-1]), F32),
            pltpu.VMEM((SAMPLE_ROWS, d_h), F32),
        ],
        input_output_aliases=aliases,
        compiler_params=pltpu.CompilerParams(
            dimension_semantics=("arbitrary",), vmem_limit_bytes=VMEM_LIMIT),
        name="mix_sample",
    )(*args)


def _mix_ln1(x, oa, ob, wout_ref, g1_ref, b1_ref, alpha):
    d_a = oa.shape[-1]
    mix = _dot(oa, wout_ref[0:d_a, :]) + _dot(ob, wout_ref[d_a:, :])
    return _layer_norm(alpha * x + mix, g1_ref[...], b1_ref[...])


def _ffn_blocks(x1b, wup_ref, cw_ref, cb_ref, wdown_ref, taps, d_ff):
    acc = None
    for j in range(d_ff // FFN_COLS):
        hs = []
        for base in (0, d_ff):
            cols = slice(base + j * FFN_COLS, base + (j + 1) * FFN_COLS)
            up = _dot(x1b, wup_ref[:, cols])
            m2, m1 = taps(up, cols)
            hs.append(cb_ref[:, cols] + m2 * cw_ref[0:1, cols] + m1 * cw_ref[1:2, cols]
                      + up * cw_ref[2:3, cols])
        val, gate = hs
        hh = (gate * jax.nn.sigmoid(gate)) * val
        part = _dot(hh.astype(BF16), wdown_ref[j * FFN_COLS:(j + 1) * FFN_COLS, :])
        acc = part if acc is None else acc + part
    return acc


def _ffn_prompt_kernel(x_ref, oa_ref, ob_ref, wout_ref, g1_ref, b1_ref, wup_ref, cw_ref, cb_ref,
                       wdown_ref, g2_ref, b2_ref, y_ref, cache_ref,
                       carry_ref, up_ref, hh_ref, acc_ref, x1_ref, x1b_ref, *, tile, alpha, d_ff):
    t = pl.program_id(1)
    lead = SUBLANES
    nb = d_ff // FFN_COLS
    sub = x1_ref.shape[1]
    subs = tile // sub
    slabs = FFN_COLS // LANES
    half_rows = FFN_ROWS // 2

    @pl.when(t == 0)
    def _():
        carry_ref[...] = jnp.zeros_like(carry_ref)

    def mix_ln1(s):
        rows = slice(s * sub, (s + 1) * sub)
        d_a = oa_ref.shape[-1]
        acc_ref[s] = _dot(oa_ref[rows, :], wout_ref[0:d_a, :]) + _dot(ob_ref[rows, :], wout_ref[d_a:, :])
        for r in range(0, sub, FFN_ROWS):
            rr = slice(r, r + FFN_ROWS)
            x1 = _layer_norm(alpha * x_ref[s * sub + r:s * sub + r + FFN_ROWS, :] + acc_ref[s, rr, :],
                             g1_ref[...], b1_ref[...])
            x1_ref[s, rr, :] = x1
            x1b_ref[s, rr, :] = x1.astype(BF16)

    def ln2(s):
        for r in range(0, sub, FFN_ROWS):
            rr = slice(r, r + FFN_ROWS)
            y_ref[s * sub + r:s * sub + r + FFN_ROWS, :] = _layer_norm(
                alpha * x1_ref[s, rr, :] + acc_ref[s, rr, :], g2_ref[...], b2_ref[...])

    def halves(j):
        return (slice(j * FFN_COLS, (j + 1) * FFN_COLS),
                slice(d_ff + j * FFN_COLS, d_ff + (j + 1) * FFN_COLS))

    def up_block(slot, s, j):
        for half, cols in enumerate(halves(j)):
            up = _dot(x1b_ref[s], wup_ref[:, cols])
            for i in range(slabs):
                lanes = slice(cols.start + i * LANES, cols.start + (i + 1) * LANES)
                up_ref[slot, half * slabs + i, lead:lead + sub, :] = up[:, i * LANES:(i + 1) * LANES]
                up_ref[slot, half * slabs + i, lead - 2:lead, :] = carry_ref[:, lanes]

    def conv_block(slot, s, j):
        for i in range(slabs):
            lanes = [slice(c.start + i * LANES, c.start + (i + 1) * LANES) for c in halves(j)]
            rep = lambda a: jnp.broadcast_to(a, (half_rows, LANES))
            prm = [(rep(cw_ref[0:1, l]), rep(cw_ref[1:2, l]), rep(cw_ref[2:3, l]), rep(cb_ref[:, l]))
                   for l in lanes]
            for r in range(0, sub, FFN_ROWS):
                res = []
                for half, (w0, w1, w2, cb) in enumerate(prm):
                    tap = lambda d: up_ref[slot, half * slabs + i, pl.ds(lead + r + d, half_rows, stride=2), :]
                    em, om, e, o = tap(-2), tap(-1), tap(0), tap(1)
                    res.append((cb + em * w0 + om * w1 + e * w2, cb + om * w0 + e * w1 + o * w2))
                (ve, vo), (ge, go) = res
                hh_ref[slot, i, pl.ds(r, half_rows, stride=2), :] = (ge * jax.nn.sigmoid(ge)) * ve
                hh_ref[slot, i, pl.ds(r + 1, half_rows, stride=2), :] = (go * jax.nn.sigmoid(go)) * vo
            for half, l in enumerate(lanes):
                carry_ref[:, l] = up_ref[slot, half * slabs + i, lead + sub - 2:lead + sub, :]

    def down_block(slot, s, j):
        hh = jnp.concatenate([hh_ref[slot, i] for i in range(slabs)], axis=-1).astype(BF16)
        part = _dot(hh, wdown_ref[j * FFN_COLS:(j + 1) * FFN_COLS, :])
        if j == 0:
            acc_ref[s] = part
        else:
            acc_ref[s] += part

    for s in range(subs):
        mix_ln1(s)
    blocks = [(s, j) for s in range(subs) for j in range(nb)]
    up_block(0, *blocks[0])
    for n, (s, j) in enumerate(blocks):
        if n + 1 < len(blocks):
            up_block((n + 1) % 2, *blocks[n + 1])
        conv_block(n % 2, s, j)
        down_block(n % 2, s, j)
        if j == nb - 1:
            ln2(s)
    cache_ref[...] = carry_ref[...]


def _ffn_sample_kernel(x_ref, oa_ref, ob_ref, m2_ref, m1_ref, wout_ref, g1_ref, b1_ref, wup_ref, cw_ref,
                       cb_ref, wdown_ref, g2_ref, b2_ref, up_all_ref, y_ref, up_ref, *, alpha, d_ff):
    del up_all_ref
    x1 = _mix_ln1(x_ref[...], oa_ref[...], ob_ref[...], wout_ref, g1_ref, b1_ref, alpha)

    def taps(up, cols):
        up_ref[:, cols] = up
        return m2_ref[:, cols], m1_ref[:, cols]

    ff = _ffn_blocks(x1.astype(BF16), wup_ref, cw_ref, cb_ref, wdown_ref, taps, d_ff)
    y_ref[...] = _layer_norm(alpha * x1 + ff, g2_ref[...], b2_ref[...])


def _resident(a, layer, grid_rank):
    idx = (layer,) + (0,) * (a.ndim - 1)
    index_map = {1: lambda i: idx, 2: lambda b, t: idx}[grid_rank]
    return pl.BlockSpec((None,) + a.shape[1:], index_map, pipeline_mode=pl.Buffered(1))


def _ffn_prompt(x, oa, ob, fw, layer, alpha):
    bsz, seq, d_model = x.shape
    d_ff = fw[6].shape[1]
    tile = min(FFN_TILE, seq)
    sub = min(FFN_SUB, tile)
    assert seq % tile == 0 and d_ff % FFN_COLS == 0 and tile % sub == 0 and sub % FFN_ROWS == 0
    subs = tile // sub
    tok = lambda d: pl.BlockSpec((None, tile, d), lambda b, t: (b, t, 0))
    return pl.pallas_call(
        functools.partial(_ffn_prompt_kernel, tile=tile, alpha=alpha, d_ff=d_ff),
        grid=(bsz, seq // tile),
        in_specs=[tok(d_model), tok(oa.shape[-1]), tok(ob.shape[-1])] + [_resident(a, layer, 2) for a in fw],
        out_specs=[tok(d_model), pl.BlockSpec((None, 2, 2 * d_ff), lambda b, t: (b, 0, 0))],
        out_shape=[jax.ShapeDtypeStruct(x.shape, F32),
                   jax.ShapeDtypeStruct((bsz, 2, 2 * d_ff), F32)],
        scratch_shapes=[
            pltpu.VMEM((2, 2 * d_ff), F32),
            pltpu.VMEM((2, 2 * FFN_COLS // LANES, SUBLANES + sub, LANES), F32),
            pltpu.VMEM((2, FFN_COLS // LANES, sub, LANES), F32),
            pltpu.VMEM((subs, sub, d_model), F32),
            pltpu.VMEM((subs, sub, d_model), F32),
            pltpu.VMEM((subs, sub, d_model), BF16),
        ],
        compiler_params=pltpu.CompilerParams(
            dimension_semantics=("arbitrary", "arbitrary"), vmem_limit_bytes=VMEM_LIMIT),
        name="ffn_prompt",
    )(x, oa, ob, *fw)


def _ffn_sample(x, oa, ob, tap2_all, tap1_all, fw, layer, alpha, up_all):
    n, d_model = x.shape
    d_ff = fw[6].shape[1]
    whole = lambda a: pl.BlockSpec(a.shape, lambda i: (0,) * a.ndim, pipeline_mode=pl.Buffered(1))
    args = [x, oa, ob, tap2_all, tap1_all, *fw, up_all]
    in_specs = [whole(x), whole(oa), whole(ob), _resident(tap2_all, layer, 1), _resident(tap1_all, layer, 1)]
    in_specs += [_resident(a, layer, 1) for a in fw]
    in_specs.append(pl.BlockSpec(memory_space=pl.ANY))
    aliases = {len(args) - 1: 1}
    return pl.pallas_call(
        functools.partial(_ffn_sample_kernel, alpha=alpha, d_ff=d_ff),
        grid=(1,),
        in_specs=in_specs,
        out_specs=[pl.BlockSpec(x.shape, lambda i: (0, 0)),
                   pl.BlockSpec((None,) + up_all.shape[1:], lambda i: (layer, 0, 0))],
        out_shape=[jax.ShapeDtypeStruct(x.shape, F32), jax.ShapeDtypeStruct(up_all.shape, F32)],
        input_output_aliases=aliases,
        compiler_params=pltpu.CompilerParams(
            dimension_semantics=("arbitrary",), vmem_limit_bytes=VMEM_LIMIT),
        name="ffn_sample",
    )(*args)


def _s5_params(lam_re, lam_im, log_dt, b_re, b_im, c_re, c_im, d_skip, w_glu):
    groups, n_p = lam_re.shape
    lr = jnp.minimum(lam_re, -1e-4)
    li = lam_im
    dt = jnp.exp(log_dt)[:, None]
    mag = jnp.exp(lr * dt)
    ab_re = mag * jnp.cos(li * dt)
    ab_im = mag * jnp.sin(li * dt)
    den = lr * lr + li * li
    nr = ab_re - 1.0
    coef_re = (nr * lr + ab_im * li) / den
    coef_im = (ab_im * lr - nr * li) / den
    bb_re = coef_re[..., None] * b_re - coef_im[..., None] * b_im
    bb_im = coef_re[..., None] * b_im + coef_im[..., None] * b_re

    gpb = LANES // S5_GROUP_CH
    kblocks = groups // gpb
    eye = jnp.eye(gpb, dtype=F32)

    def b_blocks(bb):
        bb = bb.reshape(kblocks, gpb, n_p, S5_GROUP_CH)
        return jnp.einsum('mgpj,gh->mgjhp', bb, eye).reshape(kblocks, LANES, gpb * n_p).astype(BF16)

    gps = LANES // n_p
    slabs = groups // gps
    gpt = (2 * LANES) // S5_GROUP_CH
    pos = (jnp.arange(slabs)[:, None] * gps + jnp.arange(gps)[None, :]) % gpt
    onehot = jax.nn.one_hot(pos, gpt, dtype=F32)

    def c_half(c):
        c = c.reshape(slabs, gps, S5_GROUP_CH, n_p)
        return jnp.einsum('sgjp,sgh->sgphj', c, onehot).reshape(slabs, gps * n_p, gpt * S5_GROUP_CH)

    cblk = jnp.concatenate([c_half(c_re), -c_half(c_im)], axis=1).astype(BF16)
    return (b_blocks(bb_re), b_blocks(bb_im), cblk,
            ab_re.reshape(1, groups * n_p), ab_im.reshape(1, groups * n_p),
            d_skip.reshape(1, -1), w_glu.astype(BF16))


def kernel(x_prompt, x_sample, state_hgrn, state_s5_re, state_s5_im, cache_ffn_conv, w_in, hgrn_lb_logits, hgrn_norm_g, s5_lambda_re, s5_lambda_im, s5_log_dt, s5_b_re, s5_b_im, s5_c_re, s5_c_im, s5_d, w_glu, w_out, ln1_g, ln1_b, w_ffn_up, ffn_conv_w, ffn_conv_b, w_ffn_down, ln2_g, ln2_b):
    depth = w_in.shape[0]
    d_h = hgrn_lb_logits.shape[-1]
    heads = d_h // HEAD_DIM
    n_dec = x_sample.shape[0]
    groups, n_p = s5_lambda_re.shape[1:]
    alpha = (2 * depth) ** 0.25

    sm = jax.nn.softmax(hgrn_lb_logits.astype(F32), axis=0)
    lower_bounds = jnp.cumsum(sm, axis=0) - sm[0:1]

    w_in_b = w_in.astype(BF16)
    s5p = jax.vmap(_s5_params)(s5_lambda_re, s5_lambda_im, s5_log_dt, s5_b_re, s5_b_im,
                               s5_c_re, s5_c_im, s5_d, w_glu)
    rows = lambda a: a.reshape(depth, 1, -1)
    fw = (w_out.astype(BF16), rows(ln1_g), rows(ln1_b), w_ffn_up.astype(BF16),
          ffn_conv_w, rows(ffn_conv_b), w_ffn_down.astype(BF16), rows(ln2_g), rows(ln2_b))
    h_re = state_s5_re.reshape(depth, n_dec, groups * n_p)
    h_im = state_s5_im.reshape(depth, n_dec, groups * n_p)
    tap2_all = cache_ffn_conv[:, :, 0, :]
    tap1_all = cache_ffn_conv[:, :, 1, :]

    xp = x_prompt
    xs = x_sample.reshape(n_dec, -1)
    outs = {k: [] for k in ("hp", "rp", "ip", "cp", "rs", "is")}
    hs_new = jnp.zeros(state_hgrn.shape, F32)
    up_new = jnp.zeros(tap1_all.shape, F32)
    for l in range(depth):
        lb = lower_bounds[l].reshape(1, d_h)
        ng = jnp.tile(hgrn_norm_g[l], heads).reshape(1, d_h)

        oa, s_new = _hgrn_prompt(xp, w_in_b, l, lb, ng)
        ob, re_new, im_new = _s5_prompt(xp, w_in_b, l, s5p)
        xp, cache_new = _ffn_prompt(xp, oa, ob, fw, l, alpha)
        outs["hp"].append(s_new)
        outs["rp"].append(re_new.reshape(-1, groups, n_p))
        outs["ip"].append(im_new.reshape(-1, groups, n_p))
        outs["cp"].append(cache_new)

        oa, ob, hs_new, re_new, im_new = _mix_sample(xs, w_in_b, l, lb, ng, state_hgrn, h_re, h_im, s5p, hs_new)
        xs, up_new = _ffn_sample(xs, oa, ob, tap2_all, tap1_all, fw, l, alpha, up_new)
        outs["rs"].append(re_new.reshape(n_dec, groups, n_p))
        outs["is"].append(im_new.reshape(n_dec, groups, n_p))

    st = {k: jnp.stack(v) for k, v in outs.items()}
    cs_new = jnp.stack([tap1_all, up_new], axis=2)
    return (xp, xs.reshape(x_sample.shape), st["hp"], st["rp"], st["ip"], st["cp"],
            hs_new, st["rs"], st["is"], cs_new)
```

```python
import functools
import math

import jax
import jax.numpy as jnp
import numpy as np
from jax import lax
from jax.experimental import pallas as pl
from jax.experimental.pallas import tpu as pltpu

F32 = jnp.float32
BF16 = jnp.bfloat16

HEAD_DIM = 128
CHUNK = 64
S5_GROUP_CH = 16
S5_STATE = 64
LANES = 128
SUBLANES = 8
LN_EPS = 1e-5
RMS_EPS = 1e-6
F_FLOOR = 1e-20
SAFE_SPAN = 60.0
VMEM_LIMIT = 56 * 1024 * 1024

HGRN_TILE = 512
HGRN_SUB = 256
S5_TILE = 128
S5_SUB = 32
S5_PAD = 4
FFN_TILE = 512
FFN_SUB = 256
FFN_COLS = 256
FFN_ROWS = 32
SAMPLE_ROWS = 16


def _dot(a, b):
    return jnp.dot(a, b, preferred_element_type=F32)


def _dot_nt(a, b):
    return lax.dot_general(a, b, (((1,), (1,)), ((), ())), preferred_element_type=F32)


def _dot_tn(a, b):
    return lax.dot_general(a, b, (((0,), (0,)), ((), ())), preferred_element_type=F32)


def _split3(x):
    hi = x.astype(BF16)
    r1 = x - hi.astype(F32)
    mid = r1.astype(BF16)
    lo = (r1 - mid.astype(F32)).astype(BF16)
    return hi, mid, lo


def _dot3(m, parts):
    return _dot(m, parts[0]) + _dot(m, parts[1]) + _dot(m, parts[2])


def _cumsum_rows(x):
    n, w = x.shape
    tiles = n // SUBLANES
    x3 = x.reshape(tiles, SUBLANES, w)
    sub = lax.broadcasted_iota(jnp.int32, (1, SUBLANES, 1), 1)
    s = 1
    while s < SUBLANES:
        x3 = x3 + jnp.where(sub >= s, pltpu.roll(x3, s, axis=1), 0.0)
        s *= 2
    outs = [x3[0]]
    for i in range(1, tiles):
        outs.append(x3[i] + outs[-1][SUBLANES - 1:SUBLANES, :])
    return jnp.concatenate(outs, axis=0)


def _layer_norm(x, g, b):
    mu = jnp.mean(x, axis=-1, keepdims=True)
    xc = x - mu
    var = jnp.mean(xc * xc, axis=-1, keepdims=True)
    return xc * lax.rsqrt(var + LN_EPS) * g + b


def _hgrn_gates(fz, lb):
    oml = 1.0 - lb
    f = lb + oml * jax.nn.sigmoid(fz)
    log_f = jnp.log(jnp.maximum(f, F_FLOOR))
    k = oml * jax.nn.sigmoid(-fz)
    return log_f, k


def _head_rms_gate(o, g, ng):
    heads = o.shape[-1] // HEAD_DIM
    outs = []
    for h in range(heads):
        oh = o[:, h * HEAD_DIM:(h + 1) * HEAD_DIM]
        ms = jnp.mean(oh * oh, axis=-1, keepdims=True)
        outs.append(oh * lax.rsqrt(ms + RMS_EPS))
    return jnp.concatenate(outs, axis=-1) * ng * jax.nn.sigmoid(g)


def _hgrn_prompt_kernel(x_ref, w_ref, lb_ref, ng_ref, lvl_ref, oa_ref, s_ref,
                        st_ref, st0_ref, proj_ref, xb_ref, *, tile, heads):
    t = pl.program_id(1)
    d_h = heads * HEAD_DIM

    @pl.when(t == 0)
    def _():
        st_ref[...] = jnp.zeros_like(st_ref)

    sub = min(HGRN_SUB, tile)

    pieces = sub // CHUNK
    piece_cols = 4 * d_h // pieces

    def project(k, j):
        rows = slice(k * sub, (k + 1) * sub)
        cols = slice(j * piece_cols, (j + 1) * piece_cols)
        if j == 0:
            xb_ref[rows, :] = x_ref[rows, :].astype(BF16)
        proj_ref[rows, cols] = _dot(xb_ref[rows, :], w_ref[:, cols])

    st0_ref[...] = st_ref[...]
    lb = lb_ref[...]
    ng = ng_ref[...]
    ri = lax.broadcasted_iota(jnp.int32, (CHUNK, CHUNK), 0)
    ci = lax.broadcasted_iota(jnp.int32, (CHUNK, CHUNK), 1)
    row = lax.broadcasted_iota(jnp.int32, (CHUNK, 1), 0)
    mid = CHUNK // 2 - 1

    def anchored_scores(q, k, b, log_f):
        bm = b[mid:mid + 1, :]
        qe = (q * jnp.exp(jnp.minimum(b - bm, SAFE_SPAN))).astype(BF16)
        ke = (k * jnp.exp(jnp.minimum(bm - b, SAFE_SPAN))).astype(BF16)
        return [jnp.where(ri >= ci, _dot_nt(qe[:, h * HEAD_DIM:(h + 1) * HEAD_DIM],
                                             ke[:, h * HEAD_DIM:(h + 1) * HEAD_DIM]), 0.0)
                for h in range(heads)]

    def hierarchical_scores(q, k, b, log_f):
        anchors = _dot3(lvl_ref[...], _split3(log_f))
        qb = q.astype(BF16)
        kb = k.astype(BF16)
        sc = [jnp.where(ri == ci, _dot_nt(qb[:, h * HEAD_DIM:(h + 1) * HEAD_DIM],
                                          kb[:, h * HEAD_DIM:(h + 1) * HEAD_DIM]), 0.0)
              for h in range(heads)]
        lvl = 0
        hs = CHUNK // 2
        while hs >= 1:
            a = anchors[lvl * CHUNK:(lvl + 1) * CHUNK, :]
            upper = (row & (2 * hs - 1)) >= hs
            qe = (q * jnp.where(upper, jnp.exp(jnp.minimum(b - a, 0.0)), 0.0)).astype(BF16)
            ke = (k * jnp.where(upper, 0.0, jnp.exp(jnp.minimum(a - b, 0.0)))).astype(BF16)
            shift = int(math.log2(2 * hs))
            same = (ri >> shift) == (ci >> shift)
            for h in range(heads):
                sl = slice(h * HEAD_DIM, (h + 1) * HEAD_DIM)
                sc[h] = sc[h] + jnp.where(same, _dot_nt(qe[:, sl], ke[:, sl]), 0.0)
            lvl += 1
            hs //= 2
        return sc

    def chunk(rows, scores_fn):
        q = proj_ref[rows, 0:d_h]
        fz = proj_ref[rows, d_h:2 * d_h]
        v = proj_ref[rows, 2 * d_h:3 * d_h]
        g = proj_ref[rows, 3 * d_h:4 * d_h]
        log_f, k = _hgrn_gates(fz, lb)
        b = _cumsum_rows(log_f)
        bm = b[mid:mid + 1, :]
        bl = b[CHUNK - 1:CHUNK, :]
        sc = scores_fn(q, k, b, log_f)
        qs = (q * jnp.exp(b)).astype(BF16)
        kl = (k * jnp.exp(bl - b)).astype(BF16)
        vb = v.astype(BF16)
        dec = jnp.exp(bl)
        outs = []
        for h in range(heads):
            sl = slice(h * HEAD_DIM, (h + 1) * HEAD_DIM)
            st = st_ref[h]
            outs.append(_dot(sc[h].astype(BF16), vb[:, sl]) + _dot_nt(qs[:, sl], st.astype(BF16)))
            st_ref[h] = st * dec[:, sl] + _dot_tn(vb[:, sl], kl[:, sl])
        o = jnp.concatenate(outs, axis=-1)
        oa_ref[rows, :] = _head_rms_gate(o, g, ng).astype(oa_ref.dtype)
        return jnp.minimum(bm, bl - bm)

    span = None
    for j in range(pieces):
        project(0, j)
    for c in range(tile // CHUNK):
        k, j = divmod(c, pieces)
        if (k + 1) * sub < tile:
            project(k + 1, j)
        worst = chunk(slice(c * CHUNK, (c + 1) * CHUNK), anchored_scores)
        span = worst if span is None else jnp.minimum(span, worst)
    span_ok = jnp.min(span) >= -SAFE_SPAN

    @pl.when(jnp.logical_not(span_ok))
    def _():
        st_ref[...] = st0_ref[...]

        def redo(c, carry):
            chunk(pl.ds(pl.multiple_of(c * CHUNK, CHUNK), CHUNK), hierarchical_scores)
            return carry

        lax.fori_loop(0, tile // CHUNK, redo, 0)

    @pl.when(t == pl.num_programs(1) - 1)
    def _():
        for h in range(heads):
            s_ref[h] = st_ref[h].T


def _anchor_matrices():
    r = np.arange(CHUNK)
    lvls = []
    hs = CHUNK // 2
    while hs >= 1:
        anchor = (r // (2 * hs)) * (2 * hs) + hs - 1
        lvls.append((r[None, :] <= anchor[:, None]).astype(np.float32))
        hs //= 2
    return jnp.asarray(np.concatenate(lvls, axis=0), BF16)


def _hgrn_prompt(x, w_in, layer, lb, ng):
    bsz, seq, d_model = x.shape
    d_h = lb.shape[-1]
    heads = d_h // HEAD_DIM
    tile = min(HGRN_TILE, seq)
    assert seq % tile == 0 and tile % CHUNK == 0
    lvl = _anchor_matrices()
    const = lambda *shape: pl.BlockSpec(shape, lambda b, t: (0,) * len(shape))
    return pl.pallas_call(
        functools.partial(_hgrn_prompt_kernel, tile=tile, heads=heads),
        grid=(bsz, seq // tile),
        in_specs=[
            pl.BlockSpec((None, tile, d_model), lambda b, t: (b, t, 0)),
            pl.BlockSpec((None, d_model, 4 * d_h), lambda b, t: (layer, 0, 0)),
            const(1, d_h),
            const(1, d_h),
            const(lvl.shape[0], CHUNK),
        ],
        out_specs=[
            pl.BlockSpec((None, tile, d_h), lambda b, t: (b, t, 0)),
            pl.BlockSpec((None, heads, HEAD_DIM, HEAD_DIM), lambda b, t: (b, 0, 0, 0)),
        ],
        out_shape=[
            jax.ShapeDtypeStruct((bsz, seq, d_h), BF16),
            jax.ShapeDtypeStruct((bsz, heads, HEAD_DIM, HEAD_DIM), F32),
        ],
        scratch_shapes=[
            pltpu.VMEM((heads, HEAD_DIM, HEAD_DIM), F32),
            pltpu.VMEM((heads, HEAD_DIM, HEAD_DIM), F32),
            pltpu.VMEM((tile, 4 * d_h), F32),
            pltpu.VMEM((tile, d_model), BF16),
        ],
        compiler_params=pltpu.CompilerParams(
            dimension_semantics=("arbitrary", "arbitrary"), vmem_limit_bytes=VMEM_LIMIT),
        name="hgrn_prompt",
    )(x, w_in, lb, ng, lvl)


def _s5_input_drive(ub, bre_ref, bim_ref, store):
    kblocks = bre_ref.shape[0]
    slabs_per_block = bre_ref.shape[2] // LANES
    for m in range(kblocks):
        um = ub[:, m * LANES:(m + 1) * LANES]
        re = _dot(um, bre_ref[m])
        im = _dot(um, bim_ref[m])
        for s in range(slabs_per_block):
            store(m * slabs_per_block + s, re[:, s * LANES:(s + 1) * LANES], im[:, s * LANES:(s + 1) * LANES])


def _s5_readout(load, c_ref, d_s5):
    slabs = c_ref.shape[0]
    tiles = d_s5 // (2 * LANES)
    per_tile = slabs // tiles
    ys = []
    for n in range(tiles):
        acc = None
        for s in range(n * per_tile, (n + 1) * per_tile):
            part = _dot(load(s), c_ref[s])
            acc = part if acc is None else acc + part
        ys.append(acc)
    return jnp.concatenate(ys, axis=-1)


def _s5_output(y, u, d_ref, wglu_ref):
    y = y + d_ref[...] * u
    y = 0.5 * y * (1.0 + lax.erf(y * math.sqrt(0.5)))
    return y * jax.nn.sigmoid(_dot(y.astype(BF16), wglu_ref[...]))


def _s5_prompt_kernel(x_ref, wu_ref, bre_ref, bim_ref, c_ref, are_ref, aim_ref, d_ref, wglu_ref,
                      ob_ref, hre_ref, him_ref, xr_ref, xi_ref, u_ref, y_ref, *, tile, sub, bsz):
    i = pl.program_id(0)
    pitch = sub + S5_PAD
    subs = tile // sub
    slabs = xr_ref.shape[1]
    d_s5 = d_ref.shape[-1]

    @pl.when(i == 0)
    def _():
        hre_ref[...] = jnp.zeros_like(hre_ref)
        him_ref[...] = jnp.zeros_like(him_ref)
        xr_ref[...] = jnp.zeros_like(xr_ref)
        xi_ref[...] = jnp.zeros_like(xi_ref)

    def drive(k):
        x = x_ref[:, k * sub:(k + 1) * sub, :].reshape(bsz * sub, x_ref.shape[-1])
        u = _dot(x.astype(BF16), wu_ref[...])
        u_ref[k] = u

        def store(slab, re, im):
            for b in range(bsz):
                xr_ref[k, slab, b * pitch:b * pitch + sub, :] = re[b * sub:(b + 1) * sub, :]
                xi_ref[k, slab, b * pitch:b * pitch + sub, :] = im[b * sub:(b + 1) * sub, :]

        _s5_input_drive(u.astype(BF16), bre_ref, bim_ref, store)

    def scan(k, hr, hi):
        for tt in range(sub):
            rows = pl.ds(tt, bsz, stride=pitch)
            for s in range(slabs):
                ar = are_ref[:, s * LANES:(s + 1) * LANES]
                ai = aim_ref[:, s * LANES:(s + 1) * LANES]
                nr = ar * hr[s] - ai * hi[s] + xr_ref[k, s, rows, :]
                ni = ar * hi[s] + ai * hr[s] + xi_ref[k, s, rows, :]
                xr_ref[k, s, rows, :] = nr
                xi_ref[k, s, rows, :] = ni
                hr[s], hi[s] = nr, ni
        return hr, hi

    def readout(k):
        load = lambda s: jnp.concatenate([xr_ref[k, s], xi_ref[k, s]], axis=-1).astype(BF16)
        y_ref[k] = _s5_readout(load, c_ref, d_s5)
        y = jnp.concatenate([y_ref[k, b * pitch:b * pitch + sub, :] for b in range(bsz)], axis=0)
        o = _s5_output(y, u_ref[k], d_ref, wglu_ref)
        ob_ref[:, k * sub:(k + 1) * sub, :] = o.reshape(bsz, sub, d_s5).astype(ob_ref.dtype)

    for k in range(subs):
        drive(k)
    hr = [hre_ref[:, s * LANES:(s + 1) * LANES] for s in range(slabs)]
    hi = [him_ref[:, s * LANES:(s + 1) * LANES] for s in range(slabs)]
    for k in range(subs):
        hr, hi = scan(k, hr, hi)
        readout(k)
    for s in range(slabs):
        hre_ref[:, s * LANES:(s + 1) * LANES] = hr[s]
        him_ref[:, s * LANES:(s + 1) * LANES] = hi[s]


def _layer_block(a, layer, grid_rank):
    idx = (layer,) + (0,) * (a.ndim - 1)
    index_map = {1: lambda i: idx, 2: lambda b, t: idx}[grid_rank]
    return pl.BlockSpec((None,) + a.shape[1:], index_map)


def _s5_prompt(x, w_in, layer, s5p):
    bsz, seq, d_model = x.shape
    d_s5 = s5p[5].shape[-1]
    n_state = s5p[3].shape[-1]
    slabs = n_state // LANES
    tile = min(S5_TILE, seq)
    sub = min(S5_SUB, tile)
    assert seq % tile == 0 and tile % sub == 0 and bsz == SUBLANES and w_in.shape[-1] % d_s5 == 0
    subs = tile // sub
    pitch = sub + S5_PAD
    u_block = w_in.shape[-1] // d_s5 - 1
    const = lambda *shape: pl.BlockSpec(shape, lambda i: (0,) * len(shape))
    return pl.pallas_call(
        functools.partial(_s5_prompt_kernel, tile=tile, sub=sub, bsz=bsz),
        grid=(seq // tile,),
        in_specs=[
            pl.BlockSpec((bsz, tile, d_model), lambda i: (0, i, 0)),
            pl.BlockSpec((None, d_model, d_s5), lambda i: (layer, 0, u_block)),
        ] + [_layer_block(a, layer, 1) for a in s5p],
        out_specs=[
            pl.BlockSpec((bsz, tile, d_s5), lambda i: (0, i, 0)),
            const(bsz, n_state), const(bsz, n_state),
        ],
        out_shape=[
            jax.ShapeDtypeStruct((bsz, seq, d_s5), BF16),
            jax.ShapeDtypeStruct((bsz, n_state), F32),
            jax.ShapeDtypeStruct((bsz, n_state), F32),
        ],
        scratch_shapes=[
            pltpu.VMEM((subs, slabs, bsz * pitch, LANES), F32),
            pltpu.VMEM((subs, slabs, bsz * pitch, LANES), F32),
            pltpu.VMEM((subs, bsz * sub, d_s5), F32),
            pltpu.VMEM((subs, bsz * pitch, d_s5), F32),
        ],
        compiler_params=pltpu.CompilerParams(
            dimension_semantics=("arbitrary",), vmem_limit_bytes=VMEM_LIMIT),
        name="s5_prompt",
    )(x, w_in, *s5p)


def _mix_sample_kernel(x_ref, w_ref, lb_ref, ng_ref, s_ref, hre_ref, him_ref,
                       bre_ref, bim_ref, c_ref, are_ref, aim_ref, d_ref, wglu_ref, so_all_ref,
                       oa_ref, ob_ref, so_ref, nre_ref, nim_ref, proj_ref, orow_ref, *, heads):
    del so_all_ref
    i = pl.program_id(0)
    d_h = heads * HEAD_DIM
    d_s5 = d_ref.shape[-1]
    slabs = are_ref.shape[-1] // LANES

    @pl.when(i == 0)
    def _():
        proj = _dot(x_ref[...].astype(BF16), w_ref[...])
        proj_ref[...] = proj
        u = proj[:, 4 * d_h:]

        def store(slab, re, im):
            sl = slice(slab * LANES, (slab + 1) * LANES)
            ar, ai = are_ref[:, sl], aim_ref[:, sl]
            hr, hi = hre_ref[:, sl], him_ref[:, sl]
            nre_ref[:, sl] = ar * hr - ai * hi + re
            nim_ref[:, sl] = ar * hi + ai * hr + im

        _s5_input_drive(u.astype(BF16), bre_ref, bim_ref, store)
        load = lambda s: jnp.concatenate(
            [nre_ref[:, s * LANES:(s + 1) * LANES], nim_ref[:, s * LANES:(s + 1) * LANES]],
            axis=-1).astype(BF16)
        y = _s5_readout(load, c_ref, d_s5)
        ob_ref[...] = _s5_output(y, u, d_ref, wglu_ref).astype(ob_ref.dtype)

    r0 = pl.multiple_of(i * SAMPLE_ROWS, SAMPLE_ROWS)
    rows = proj_ref[pl.ds(r0, SAMPLE_ROWS), :]
    q = rows[:, 0:d_h]
    fz = rows[:, d_h:2 * d_h]
    v = rows[:, 2 * d_h:3 * d_h]
    g = rows[:, 3 * d_h:4 * d_h]
    log_f, k = _hgrn_gates(fz, lb_ref[...])
    f = jnp.exp(log_f)
    per = heads * SAMPLE_ROWS
    pieces = [a[:, h * HEAD_DIM:(h + 1) * HEAD_DIM] for a in (f, k) for h in range(heads)]
    assert 2 * per <= HEAD_DIM
    if 2 * per < HEAD_DIM:
        pieces.append(jnp.zeros((HEAD_DIM - 2 * per, HEAD_DIM), F32))
    cols = jnp.concatenate(pieces, axis=0).T
    qb = q.astype(BF16)
    for h in range(heads):
        hl = slice(h * HEAD_DIM, (h + 1) * HEAD_DIM)
        for r in range(SAMPLE_ROWS):
            j = h * SAMPLE_ROWS + r
            fc = cols[:, j:j + 1]
            kc = cols[:, per + j:per + j + 1]
            sn = fc * s_ref[r, h] + kc * v[r:r + 1, hl]
            so_ref[r, h] = sn
            orow_ref[r:r + 1, hl] = _dot(qb[:, hl], sn.astype(BF16))[r:r + 1, :]
    oa_ref[pl.ds(r0, SAMPLE_ROWS), :] = _head_rms_gate(orow_ref[...], g, ng_ref[...]).astype(oa_ref.dtype)


def _mix_sample(x, w_in, layer, lb, ng, s_all, h_re, h_im, s5p, s_new_all):
    n, d_model = x.shape
    d_h = lb.shape[-1]
    heads = d_h // HEAD_DIM
    d_s5 = s5p[5].shape[-1]
    n_state = s5p[3].shape[-1]
    assert n % SAMPLE_ROWS == 0
    const = lambda *shape: pl.BlockSpec(shape, lambda i: (0,) * len(shape))
    state_spec = pl.BlockSpec((None, SAMPLE_ROWS, heads, HEAD_DIM, HEAD_DIM), lambda i: (layer, i, 0, 0, 0))
    args = [x, w_in, lb, ng, s_all, h_re, h_im, *s5p, s_new_all]
    in_specs = [const(n, d_model), _layer_block(w_in, layer, 1), const(1, d_h), const(1, d_h),
                state_spec, _layer_block(h_re, layer, 1), _layer_block(h_im, layer, 1)]
    in_specs += [_layer_block(a, layer, 1) for a in s5p]
    in_specs.append(pl.BlockSpec(memory_space=pl.ANY))
    aliases = {len(args) - 1: 2}
    return pl.pallas_call(
        functools.partial(_mix_sample_kernel, heads=heads),
        grid=(n // SAMPLE_ROWS,),
        in_specs=in_specs,
        out_specs=[
            const(n, d_h), const(n, d_s5), state_spec, const(n, n_state), const(n, n_state),
        ],
        out_shape=[
            jax.ShapeDtypeStruct((n, d_h), BF16),
            jax.ShapeDtypeStruct((n, d_s5), BF16),
            jax.ShapeDtypeStruct(s_all.shape, F32),
            jax.ShapeDtypeStruct((n, n_state), F32),
            jax.ShapeDtypeStruct((n, n_state), F32),
        ],
        scratch_shapes=[
            pltpu.VMEM((n, w_in.shape[-1]), F32),
            pltpu.VMEM((SAMPLE_ROWS, d_h), F32),
        ],
        input_output_aliases=aliases,
        compiler_params=pltpu.CompilerParams(
            dimension_semantics=("arbitrary",), vmem_limit_bytes=VMEM_LIMIT),
        name="mix_sample",
    )(*args)


def _mix_ln1(x, oa, ob, wout_ref, g1_ref, b1_ref, alpha):
    d_a = oa.shape[-1]
    mix = _dot(oa, wout_ref[0:d_a, :]) + _dot(ob, wout_ref[d_a:, :])
    return _layer_norm(alpha * x + mix, g1_ref[...], b1_ref[...])


def _ffn_blocks(x1b, wup_ref, cw_ref, cb_ref, wdown_ref, taps, d_ff):
    acc = None
    for j in range(d_ff // FFN_COLS):
        hs = []
        for base in (0, d_ff):
            cols = slice(base + j * FFN_COLS, base + (j + 1) * FFN_COLS)
            up = _dot(x1b, wup_ref[:, cols])
            m2, m1 = taps(up, cols)
            hs.append(cb_ref[:, cols] + m2 * cw_ref[0:1, cols] + m1 * cw_ref[1:2, cols]
                      + up * cw_ref[2:3, cols])
        val, gate = hs
        hh = (gate * jax.nn.sigmoid(gate)) * val
        part = _dot(hh.astype(BF16), wdown_ref[j * FFN_COLS:(j + 1) * FFN_COLS, :])
        acc = part if acc is None else acc + part
    return acc


def _ffn_prompt_kernel(x_ref, oa_ref, ob_ref, wout_ref, g1_ref, b1_ref, wup_ref, cw_ref, cb_ref,
                       wdown_ref, g2_ref, b2_ref, y_ref, cache_ref,
                       carry_ref, up_ref, hh_ref, acc_ref, x1_ref, x1b_ref, *, tile, alpha, d_ff):
    t = pl.program_id(1)
    lead = SUBLANES
    nb = d_ff // FFN_COLS
    sub = x1_ref.shape[1]
    subs = tile // sub
    slabs = FFN_COLS // LANES
    half_rows = FFN_ROWS // 2

    @pl.when(t == 0)
    def _():
        carry_ref[...] = jnp.zeros_like(carry_ref)

    def mix_ln1(s):
        rows = slice(s * sub, (s + 1) * sub)
        d_a = oa_ref.shape[-1]
        acc_ref[s] = _dot(oa_ref[rows, :], wout_ref[0:d_a, :]) + _dot(ob_ref[rows, :], wout_ref[d_a:, :])
        for r in range(0, sub, FFN_ROWS):
            rr = slice(r, r + FFN_ROWS)
            x1 = _layer_norm(alpha * x_ref[s * sub + r:s * sub + r + FFN_ROWS, :] + acc_ref[s, rr, :],
                             g1_ref[...], b1_ref[...])
            x1_ref[s, rr, :] = x1
            x1b_ref[s, rr, :] = x1.astype(BF16)

    def ln2(s):
        for r in range(0, sub, FFN_ROWS):
            rr = slice(r, r + FFN_ROWS)
            y_ref[s * sub + r:s * sub + r + FFN_ROWS, :] = _layer_norm(
                alpha * x1_ref[s, rr, :] + acc_ref[s, rr, :], g2_ref[...], b2_ref[...])

    def halves(j):
        return (slice(j * FFN_COLS, (j + 1) * FFN_COLS),
                slice(d_ff + j * FFN_COLS, d_ff + (j + 1) * FFN_COLS))

    def up_block(slot, s, j):
        for half, cols in enumerate(halves(j)):
            up = _dot(x1b_ref[s], wup_ref[:, cols])
            for i in range(slabs):
                lanes = slice(cols.start + i * LANES, cols.start + (i + 1) * LANES)
                up_ref[slot, half * slabs + i, lead:lead + sub, :] = up[:, i * LANES:(i + 1) * LANES]
                up_ref[slot, half * slabs + i, lead - 2:lead, :] = carry_ref[:, lanes]

    def conv_block(slot, s, j):
        for i in range(slabs):
            lanes = [slice(c.start + i * LANES, c.start + (i + 1) * LANES) for c in halves(j)]
            rep = lambda a: jnp.broadcast_to(a, (half_rows, LANES))
            prm = [(rep(cw_ref[0:1, l]), rep(cw_ref[1:2, l]), rep(cw_ref[2:3, l]), rep(cb_ref[:, l]))
                   for l in lanes]
            for r in range(0, sub, FFN_ROWS):
                res = []
                for half, (w0, w1, w2, cb) in enumerate(prm):
                    tap = lambda d: up_ref[slot, half * slabs + i, pl.ds(lead + r + d, half_rows, stride=2), :]
                    em, om, e, o = tap(-2), tap(-1), tap(0), tap(1)
                    res.append((cb + em * w0 + om * w1 + e * w2, cb + om * w0 + e * w1 + o * w2))
                (ve, vo), (ge, go) = res
                hh_ref[slot, i, pl.ds(r, half_rows, stride=2), :] = (ge * jax.nn.sigmoid(ge)) * ve
                hh_ref[slot, i, pl.ds(r + 1, half_rows, stride=2), :] = (go * jax.nn.sigmoid(go)) * vo
            for half, l in enumerate(lanes):
                carry_ref[:, l] = up_ref[slot, half * slabs + i, lead + sub - 2:lead + sub, :]

    def down_block(slot, s, j):
        hh = jnp.concatenate([hh_ref[slot, i] for i in range(slabs)], axis=-1).astype(BF16)
        part = _dot(hh, wdown_ref[j * FFN_COLS:(j + 1) * FFN_COLS, :])
        if j == 0:
            acc_ref[s] = part
        else:
            acc_ref[s] += part

    for s in range(subs):
        mix_ln1(s)
    blocks = [(s, j) for s in range(subs) for j in range(nb)]
    up_block(0, *blocks[0])
    for n, (s, j) in enumerate(blocks):
        if n + 1 < len(blocks):
            up_block((n + 1) % 2, *blocks[n + 1])
        conv_block(n % 2, s, j)
        down_block(n % 2, s, j)
        if j == nb - 1:
            ln2(s)
    cache_ref[...] = carry_ref[...]


def _ffn_sample_kernel(x_ref, oa_ref, ob_ref, m2_ref, m1_ref, wout_ref, g1_ref, b1_ref, wup_ref, cw_ref,
                       cb_ref, wdown_ref, g2_ref, b2_ref, up_all_ref, y_ref, up_ref, *, alpha, d_ff):
    del up_all_ref
    x1 = _mix_ln1(x_ref[...], oa_ref[...], ob_ref[...], wout_ref, g1_ref, b1_ref, alpha)

    def taps(up, cols):
        up_ref[:, cols] = up
        return m2_ref[:, cols], m1_ref[:, cols]

    ff = _ffn_blocks(x1.astype(BF16), wup_ref, cw_ref, cb_ref, wdown_ref, taps, d_ff)
    y_ref[...] = _layer_norm(alpha * x1 + ff, g2_ref[...], b2_ref[...])


def _resident(a, layer, grid_rank):
    idx = (layer,) + (0,) * (a.ndim - 1)
    index_map = {1: lambda i: idx, 2: lambda b, t: idx}[grid_rank]
    return pl.BlockSpec((None,) + a.shape[1:], index_map, pipeline_mode=pl.Buffered(1))


def _ffn_prompt(x, oa, ob, fw, layer, alpha):
    bsz, seq, d_model = x.shape
    d_ff = fw[6].shape[1]
    tile = min(FFN_TILE, seq)
    sub = min(FFN_SUB, tile)
    assert seq % tile == 0 and d_ff % FFN_COLS == 0 and tile % sub == 0 and sub % FFN_ROWS == 0
    subs = tile // sub
    tok = lambda d: pl.BlockSpec((None, tile, d), lambda b, t: (b, t, 0))
    return pl.pallas_call(
        functools.partial(_ffn_prompt_kernel, tile=tile, alpha=alpha, d_ff=d_ff),
        grid=(bsz, seq // tile),
        in_specs=[tok(d_model), tok(oa.shape[-1]), tok(ob.shape[-1])] + [_resident(a, layer, 2) for a in fw],
        out_specs=[tok(d_model), pl.BlockSpec((None, 2, 2 * d_ff), lambda b, t: (b, 0, 0))],
        out_shape=[jax.ShapeDtypeStruct(x.shape, F32),
                   jax.ShapeDtypeStruct((bsz, 2, 2 * d_ff), F32)],
        scratch_shapes=[
            pltpu.VMEM((2, 2 * d_ff), F32),
            pltpu.VMEM((2, 2 * FFN_COLS // LANES, SUBLANES + sub, LANES), F32),
            pltpu.VMEM((2, FFN_COLS // LANES, sub, LANES), F32),
            pltpu.VMEM((subs, sub, d_model), F32),
            pltpu.VMEM((subs, sub, d_model), F32),
            pltpu.VMEM((subs, sub, d_model), BF16),
        ],
        compiler_params=pltpu.CompilerParams(
            dimension_semantics=("arbitrary", "arbitrary"), vmem_limit_bytes=VMEM_LIMIT),
        name="ffn_prompt",
    )(x, oa, ob, *fw)


def _ffn_sample(x, oa, ob, tap2_all, tap1_all, fw, layer, alpha, up_all):
    n, d_model = x.shape
    d_ff = fw[6].shape[1]
    whole = lambda a: pl.BlockSpec(a.shape, lambda i: (0,) * a.ndim, pipeline_mode=pl.Buffered(1))
    args = [x, oa, ob, tap2_all, tap1_all, *fw, up_all]
    in_specs = [whole(x), whole(oa), whole(ob), _resident(tap2_all, layer, 1), _resident(tap1_all, layer, 1)]
    in_specs += [_resident(a, layer, 1) for a in fw]
    in_specs.append(pl.BlockSpec(memory_space=pl.ANY))
    aliases = {len(args) - 1: 1}
    return pl.pallas_call(
        functools.partial(_ffn_sample_kernel, alpha=alpha, d_ff=d_ff),
        grid=(1,),
        in_specs=in_specs,
        out_specs=[pl.BlockSpec(x.shape, lambda i: (0, 0)),
                   pl.BlockSpec((None,) + up_all.shape[1:], lambda i: (layer, 0, 0))],
        out_shape=[jax.ShapeDtypeStruct(x.shape, F32), jax.ShapeDtypeStruct(up_all.shape, F32)],
        input_output_aliases=aliases,
        compiler_params=pltpu.CompilerParams(
            dimension_semantics=("arbitrary",), vmem_limit_bytes=VMEM_LIMIT),
        name="ffn_sample",
    )(*args)


def _s5_params(lam_re, lam_im, log_dt, b_re, b_im, c_re, c_im, d_skip, w_glu):
    groups, n_p = lam_re.shape
    lr = jnp.minimum(lam_re, -1e-4)
    li = lam_im
    dt = jnp.exp(log_dt)[:, None]
    mag = jnp.exp(lr * dt)
    ab_re = mag * jnp.cos(li * dt)
    ab_im = mag * jnp.sin(li * dt)
    den = lr * lr + li * li
    nr = ab_re - 1.0
    coef_re = (nr * lr + ab_im * li) / den
    coef_im = (ab_im * lr - nr * li) / den
    bb_re = coef_re[..., None] * b_re - coef_im[..., None] * b_im
    bb_im = coef_re[..., None] * b_im + coef_im[..., None] * b_re

    gpb = LANES // S5_GROUP_CH
    kblocks = groups // gpb
    eye = jnp.eye(gpb, dtype=F32)

    def b_blocks(bb):
        bb = bb.reshape(kblocks, gpb, n_p, S5_GROUP_CH)
        return jnp.einsum('mgpj,gh->mgjhp', bb, eye).reshape(kblocks, LANES, gpb * n_p).astype(BF16)

    gps = LANES // n_p
    slabs = groups // gps
    gpt = (2 * LANES) // S5_GROUP_CH
    pos = (jnp.arange(slabs)[:, None] * gps + jnp.arange(gps)[None, :]) % gpt
    onehot = jax.nn.one_hot(pos, gpt, dtype=F32)

    def c_half(c):
        c = c.reshape(slabs, gps, S5_GROUP_CH, n_p)
        return jnp.einsum('sgjp,sgh->sgphj', c, onehot).reshape(slabs, gps * n_p, gpt * S5_GROUP_CH)

    cblk = jnp.concatenate([c_half(c_re), -c_half(c_im)], axis=1).astype(BF16)
    return (b_blocks(bb_re), b_blocks(bb_im), cblk,
            ab_re.reshape(1, groups * n_p), ab_im.reshape(1, groups * n_p),
            d_skip.reshape(1, -1), w_glu.astype(BF16))


def kernel(x_prompt, x_sample, state_hgrn, state_s5_re, state_s5_im, cache_ffn_conv, w_in, hgrn_lb_logits, hgrn_norm_g, s5_lambda_re, s5_lambda_im, s5_log_dt, s5_b_re, s5_b_im, s5_c_re, s5_c_im, s5_d, w_glu, w_out, ln1_g, ln1_b, w_ffn_up, ffn_conv_w, ffn_conv_b, w_ffn_down, ln2_g, ln2_b):
    depth = w_in.shape[0]
    d_h = hgrn_lb_logits.shape[-1]
    heads = d_h // HEAD_DIM
    n_dec = x_sample.shape[0]
    groups, n_p = s5_lambda_re.shape[1:]
    alpha = (2 * depth) ** 0.25

    sm = jax.nn.softmax(hgrn_lb_logits.astype(F32), axis=0)
    lower_bounds = jnp.cumsum(sm, axis=0) - sm[0:1]

    w_in_b = w_in.astype(BF16)
    s5p = jax.vmap(_s5_params)(s5_lambda_re, s5_lambda_im, s5_log_dt, s5_b_re, s5_b_im,
                               s5_c_re, s5_c_im, s5_d, w_glu)
    rows = lambda a: a.reshape(depth, 1, -1)
    fw = (w_out.astype(BF16), rows(ln1_g), rows(ln1_b), w_ffn_up.astype(BF16),
          ffn_conv_w, rows(ffn_conv_b), w_ffn_down.astype(BF16), rows(ln2_g), rows(ln2_b))
    h_re = state_s5_re.reshape(depth, n_dec, groups * n_p)
    h_im = state_s5_im.reshape(depth, n_dec, groups * n_p)
    tap2_all = cache_ffn_conv[:, :, 0, :]
    tap1_all = cache_ffn_conv[:, :, 1, :]

    xp = x_prompt
    xs = x_sample.reshape(n_dec, -1)
    outs = {k: [] for k in ("hp", "rp", "ip", "cp", "rs", "is")}
    hs_new = jnp.zeros(state_hgrn.shape, F32)
    up_new = jnp.zeros(tap1_all.shape, F32)
    for l in range(depth):
        lb = lower_bounds[l].reshape(1, d_h)
        ng = jnp.tile(hgrn_norm_g[l], heads).reshape(1, d_h)

        oa, s_new = _hgrn_prompt(xp, w_in_b, l, lb, ng)
        ob, re_new, im_new = _s5_prompt(xp, w_in_b, l, s5p)
        xp, cache_new = _ffn_prompt(xp, oa, ob, fw, l, alpha)
        outs["hp"].append(s_new)
        outs["rp"].append(re_new.reshape(-1, groups, n_p))
        outs["ip"].append(im_new.reshape(-1, groups, n_p))
        outs["cp"].append(cache_new)

        oa, ob, hs_new, re_new, im_new = _mix_sample(xs, w_in_b, l, lb, ng, state_hgrn, h_re, h_im, s5p, hs_new)
        xs, up_new = _ffn_sample(xs, oa, ob, tap2_all, tap1_all, fw, l, alpha, up_new)
        outs["rs"].append(re_new.reshape(n_dec, groups, n_p))
        outs["is"].append(im_new.reshape(n_dec, groups, n_p))

    st = {k: jnp.stack(v) for k, v in outs.items()}
    cs_new = jnp.stack([tap1_all, up_new], axis=2)
    return (xp, xs.reshape(x_sample.shape), st["hp"], st["rp"], st["ip"], st["cp"],
            hs_new, st["rs"], st["is"], cs_new)
```

```python
import functools
import math

import jax
import jax.numpy as jnp
import numpy as np
from jax import lax
from jax.experimental import pallas as pl
from jax.experimental.pallas import tpu as pltpu

F32 = jnp.float32
BF16 = jnp.bfloat16

HEAD_DIM = 128
CHUNK = 64
S5_GROUP_CH = 16
S5_STATE = 64
LANES = 128
SUBLANES = 8
LN_EPS = 1e-5
RMS_EPS = 1e-6
F_FLOOR = 1e-20
SAFE_SPAN = 60.0
VMEM_LIMIT = 56 * 1024 * 1024

HGRN_TILE = 512
HGRN_SUB = 256
S5_TILE = 128
S5_SUB = 64
S5_PAD = 4
FFN_TILE = 512
FFN_SUB = 256
FFN_COLS = 256
FFN_ROWS = 32
SAMPLE_ROWS = 16


def _dot(a, b):
    return jnp.dot(a, b, preferred_element_type=F32)


def _dot_nt(a, b):
    return lax.dot_general(a, b, (((1,), (1,)), ((), ())), preferred_element_type=F32)


def _dot_tn(a, b):
    return lax.dot_general(a, b, (((0,), (0,)), ((), ())), preferred_element_type=F32)


def _split3(x):
    hi = x.astype(BF16)
    r1 = x - hi.astype(F32)
    mid = r1.astype(BF16)
    lo = (r1 - mid.astype(F32)).astype(BF16)
    return hi, mid, lo


def _dot3(m, parts):
    return _dot(m, parts[0]) + _dot(m, parts[1]) + _dot(m, parts[2])


def _cumsum_rows(x):
    n, w = x.shape
    tiles = n // SUBLANES
    x3 = x.reshape(tiles, SUBLANES, w)
    sub = lax.broadcasted_iota(jnp.int32, (1, SUBLANES, 1), 1)
    s = 1
    while s < SUBLANES:
        x3 = x3 + jnp.where(sub >= s, pltpu.roll(x3, s, axis=1), 0.0)
        s *= 2
    outs = [x3[0]]
    for i in range(1, tiles):
        outs.append(x3[i] + outs[-1][SUBLANES - 1:SUBLANES, :])
    return jnp.concatenate(outs, axis=0)


def _layer_norm(x, g, b):
    mu = jnp.mean(x, axis=-1, keepdims=True)
    xc = x - mu
    var = jnp.mean(xc * xc, axis=-1, keepdims=True)
    return xc * lax.rsqrt(var + LN_EPS) * g + b


def _hgrn_gates(fz, lb):
    oml = 1.0 - lb
    f = lb + oml * jax.nn.sigmoid(fz)
    log_f = jnp.log(jnp.maximum(f, F_FLOOR))
    k = oml * jax.nn.sigmoid(-fz)
    return log_f, k


def _head_rms_gate(o, g, ng):
    heads = o.shape[-1] // HEAD_DIM
    outs = []
    for h in range(heads):
        oh = o[:, h * HEAD_DIM:(h + 1) * HEAD_DIM]
        ms = jnp.mean(oh * oh, axis=-1, keepdims=True)
        outs.append(oh * lax.rsqrt(ms + RMS_EPS))
    return jnp.concatenate(outs, axis=-1) * ng * jax.nn.sigmoid(g)


def _hgrn_prompt_kernel(x_ref, w_ref, lb_ref, ng_ref, lvl_ref, oa_ref, s_ref,
                        st_ref, st0_ref, proj_ref, xb_ref, *, tile, heads):
    t = pl.program_id(1)
    d_h = heads * HEAD_DIM

    @pl.when(t == 0)
    def _():
        st_ref[...] = jnp.zeros_like(st_ref)

    sub = min(HGRN_SUB, tile)

    pieces = sub // CHUNK
    piece_cols = 4 * d_h // pieces

    def project(k, j):
        rows = slice(k * sub, (k + 1) * sub)
        cols = slice(j * piece_cols, (j + 1) * piece_cols)
        if j == 0:
            xb_ref[rows, :] = x_ref[rows, :].astype(BF16)
        proj_ref[rows, cols] = _dot(xb_ref[rows, :], w_ref[:, cols])

    st0_ref[...] = st_ref[...]
    lb = lb_ref[...]
    ng = ng_ref[...]
    ri = lax.broadcasted_iota(jnp.int32, (CHUNK, CHUNK), 0)
    ci = lax.broadcasted_iota(jnp.int32, (CHUNK, CHUNK), 1)
    row = lax.broadcasted_iota(jnp.int32, (CHUNK, 1), 0)
    mid = CHUNK // 2 - 1

    def anchored_scores(q, k, b, log_f):
        bm = b[mid:mid + 1, :]
        qe = (q * jnp.exp(jnp.minimum(b - bm, SAFE_SPAN))).astype(BF16)
        ke = (k * jnp.exp(jnp.minimum(bm - b, SAFE_SPAN))).astype(BF16)
        return [jnp.where(ri >= ci, _dot_nt(qe[:, h * HEAD_DIM:(h + 1) * HEAD_DIM],
                                             ke[:, h * HEAD_DIM:(h + 1) * HEAD_DIM]), 0.0)
                for h in range(heads)]

    def hierarchical_scores(q, k, b, log_f):
        anchors = _dot3(lvl_ref[...], _split3(log_f))
        qb = q.astype(BF16)
        kb = k.astype(BF16)
        sc = [jnp.where(ri == ci, _dot_nt(qb[:, h * HEAD_DIM:(h + 1) * HEAD_DIM],
                                          kb[:, h * HEAD_DIM:(h + 1) * HEAD_DIM]), 0.0)
              for h in range(heads)]
        lvl = 0
        hs = CHUNK // 2
        while hs >= 1:
            a = anchors[lvl * CHUNK:(lvl + 1) * CHUNK, :]
            upper = (row & (2 * hs - 1)) >= hs
            qe = (q * jnp.where(upper, jnp.exp(jnp.minimum(b - a, 0.0)), 0.0)).astype(BF16)
            ke = (k * jnp.where(upper, 0.0, jnp.exp(jnp.minimum(a - b, 0.0)))).astype(BF16)
            shift = int(math.log2(2 * hs))
            same = (ri >> shift) == (ci >> shift)
            for h in range(heads):
                sl = slice(h * HEAD_DIM, (h + 1) * HEAD_DIM)
                sc[h] = sc[h] + jnp.where(same, _dot_nt(qe[:, sl], ke[:, sl]), 0.0)
            lvl += 1
            hs //= 2
        return sc

    def chunk(rows, scores_fn):
        q = proj_ref[rows, 0:d_h]
        fz = proj_ref[rows, d_h:2 * d_h]
        v = proj_ref[rows, 2 * d_h:3 * d_h]
        g = proj_ref[rows, 3 * d_h:4 * d_h]
        log_f, k = _hgrn_gates(fz, lb)
        b = _cumsum_rows(log_f)
        bm = b[mid:mid + 1, :]
        bl = b[CHUNK - 1:CHUNK, :]
        sc = scores_fn(q, k, b, log_f)
        qs = (q * jnp.exp(b)).astype(BF16)
        kl = (k * jnp.exp(bl - b)).astype(BF16)
        vb = v.astype(BF16)
        dec = jnp.exp(bl)
        outs = []
        for h in range(heads):
            sl = slice(h * HEAD_DIM, (h + 1) * HEAD_DIM)
            st = st_ref[h]
            outs.append(_dot(sc[h].astype(BF16), vb[:, sl]) + _dot_nt(qs[:, sl], st.astype(BF16)))
            st_ref[h] = st * dec[:, sl] + _dot_tn(vb[:, sl], kl[:, sl])
        o = jnp.concatenate(outs, axis=-1)
        oa_ref[rows, :] = _head_rms_gate(o, g, ng).astype(oa_ref.dtype)
        return jnp.minimum(bm, bl - bm)

    span = None
    for j in range(pieces):
        project(0, j)
    for c in range(tile // CHUNK):
        k, j = divmod(c, pieces)
        if (k + 1) * sub < tile:
            project(k + 1, j)
        worst = chunk(slice(c * CHUNK, (c + 1) * CHUNK), anchored_scores)
        span = worst if span is None else jnp.minimum(span, worst)
    span_ok = jnp.min(span) >= -SAFE_SPAN

    @pl.when(jnp.logical_not(span_ok))
    def _():
        st_ref[...] = st0_ref[...]

        def redo(c, carry):
            chunk(pl.ds(pl.multiple_of(c * CHUNK, CHUNK), CHUNK), hierarchical_scores)
            return carry

        lax.fori_loop(0, tile // CHUNK, redo, 0)

    @pl.when(t == pl.num_programs(1) - 1)
    def _():
        for h in range(heads):
            s_ref[h] = st_ref[h].T


def _anchor_matrices():
    r = np.arange(CHUNK)
    lvls = []
    hs = CHUNK // 2
    while hs >= 1:
        anchor = (r // (2 * hs)) * (2 * hs) + hs - 1
        lvls.append((r[None, :] <= anchor[:, None]).astype(np.float32))
        hs //= 2
    return jnp.asarray(np.concatenate(lvls, axis=0), BF16)


def _hgrn_prompt(x, w_in, layer, lb, ng):
    bsz, seq, d_model = x.shape
    d_h = lb.shape[-1]
    heads = d_h // HEAD_DIM
    tile = min(HGRN_TILE, seq)
    assert seq % tile == 0 and tile % CHUNK == 0
    lvl = _anchor_matrices()
    const = lambda *shape: pl.BlockSpec(shape, lambda b, t: (0,) * len(shape))
    return pl.pallas_call(
        functools.partial(_hgrn_prompt_kernel, tile=tile, heads=heads),
        grid=(bsz, seq // tile),
        in_specs=[
            pl.BlockSpec((None, tile, d_model), lambda b, t: (b, t, 0)),
            pl.BlockSpec((None, d_model, 4 * d_h), lambda b, t: (layer, 0, 0)),
            const(1, d_h),
            const(1, d_h),
            const(lvl.shape[0], CHUNK),
        ],
        out_specs=[
            pl.BlockSpec((None, tile, d_h), lambda b, t: (b, t, 0)),
            pl.BlockSpec((None, heads, HEAD_DIM, HEAD_DIM), lambda b, t: (b, 0, 0, 0)),
        ],
        out_shape=[
            jax.ShapeDtypeStruct((bsz, seq, d_h), BF16),
            jax.ShapeDtypeStruct((bsz, heads, HEAD_DIM, HEAD_DIM), F32),
        ],
        scratch_shapes=[
            pltpu.VMEM((heads, HEAD_DIM, HEAD_DIM), F32),
            pltpu.VMEM((heads, HEAD_DIM, HEAD_DIM), F32),
            pltpu.VMEM((tile, 4 * d_h), F32),
            pltpu.VMEM((tile, d_model), BF16),
        ],
        compiler_params=pltpu.CompilerParams(
            dimension_semantics=("arbitrary", "arbitrary"), vmem_limit_bytes=VMEM_LIMIT),
        name="hgrn_prompt",
    )(x, w_in, lb, ng, lvl)


def _s5_input_drive(ub, bre_ref, bim_ref, store):
    kblocks = bre_ref.shape[0]
    slabs_per_block = bre_ref.shape[2] // LANES
    for m in range(kblocks):
        um = ub[:, m * LANES:(m + 1) * LANES]
        re = _dot(um, bre_ref[m])
        im = _dot(um, bim_ref[m])
        for s in range(slabs_per_block):
            store(m * slabs_per_block + s, re[:, s * LANES:(s + 1) * LANES], im[:, s * LANES:(s + 1) * LANES])


def _s5_readout(load, c_ref, d_s5):
    slabs = c_ref.shape[0]
    tiles = d_s5 // (2 * LANES)
    per_tile = slabs // tiles
    ys = []
    for n in range(tiles):
        acc = None
        for s in range(n * per_tile, (n + 1) * per_tile):
            part = _dot(load(s), c_ref[s])
            acc = part if acc is None else acc + part
        ys.append(acc)
    return jnp.concatenate(ys, axis=-1)


def _s5_output(y, u, d_ref, wglu_ref):
    y = y + d_ref[...] * u
    y = 0.5 * y * (1.0 + lax.erf(y * math.sqrt(0.5)))
    return y * jax.nn.sigmoid(_dot(y.astype(BF16), wglu_ref[...]))


def _s5_prompt_kernel(x_ref, wu_ref, bre_ref, bim_ref, c_ref, are_ref, aim_ref, d_ref, wglu_ref,
                      ob_ref, hre_ref, him_ref, xr_ref, xi_ref, u_ref, y_ref, *, tile, sub, bsz):
    i = pl.program_id(0)
    pitch = sub + S5_PAD
    subs = tile // sub
    slabs = xr_ref.shape[1]
    d_s5 = d_ref.shape[-1]

    @pl.when(i == 0)
    def _():
        hre_ref[...] = jnp.zeros_like(hre_ref)
        him_ref[...] = jnp.zeros_like(him_ref)
        xr_ref[...] = jnp.zeros_like(xr_ref)
        xi_ref[...] = jnp.zeros_like(xi_ref)

    def drive(k):
        x = x_ref[:, k * sub:(k + 1) * sub, :].reshape(bsz * sub, x_ref.shape[-1])
        u = _dot(x.astype(BF16), wu_ref[...])
        u_ref[k] = u

        def store(slab, re, im):
            for b in range(bsz):
                xr_ref[k, slab, b * pitch:b * pitch + sub, :] = re[b * sub:(b + 1) * sub, :]
                xi_ref[k, slab, b * pitch:b * pitch + sub, :] = im[b * sub:(b + 1) * sub, :]

        _s5_input_drive(u.astype(BF16), bre_ref, bim_ref, store)

    def scan(k, hr, hi):
        for tt in range(sub):
            rows = pl.ds(tt, bsz, stride=pitch)
            for s in range(slabs):
                ar = are_ref[:, s * LANES:(s + 1) * LANES]
                ai = aim_ref[:, s * LANES:(s + 1) * LANES]
                nr = ar * hr[s] - ai * hi[s] + xr_ref[k, s, rows, :]
                ni = ar * hi[s] + ai * hr[s] + xi_ref[k, s, rows, :]
                xr_ref[k, s, rows, :] = nr
                xi_ref[k, s, rows, :] = ni
                hr[s], hi[s] = nr, ni
        return hr, hi

    def readout(k):
        load = lambda s: jnp.concatenate([xr_ref[k, s], xi_ref[k, s]], axis=-1).astype(BF16)
        y_ref[k] = _s5_readout(load, c_ref, d_s5)
        y = jnp.concatenate([y_ref[k, b * pitch:b * pitch + sub, :] for b in range(bsz)], axis=0)
        o = _s5_output(y, u_ref[k], d_ref, wglu_ref)
        ob_ref[:, k * sub:(k + 1) * sub, :] = o.reshape(bsz, sub, d_s5).astype(ob_ref.dtype)

    for k in range(subs):
        drive(k)
    hr = [hre_ref[:, s * LANES:(s + 1) * LANES] for s in range(slabs)]
    hi = [him_ref[:, s * LANES:(s + 1) * LANES] for s in range(slabs)]
    for k in range(subs):
        hr, hi = scan(k, hr, hi)
        readout(k)
    for s in range(slabs):
        hre_ref[:, s * LANES:(s + 1) * LANES] = hr[s]
        him_ref[:, s * LANES:(s + 1) * LANES] = hi[s]


def _layer_block(a, layer, grid_rank):
    idx = (layer,) + (0,) * (a.ndim - 1)
    index_map = {1: lambda i: idx, 2: lambda b, t: idx}[grid_rank]
    return pl.BlockSpec((None,) + a.shape[1:], index_map)


def _s5_prompt(x, w_in, layer, s5p):
    bsz, seq, d_model = x.shape
    d_s5 = s5p[5].shape[-1]
    n_state = s5p[3].shape[-1]
    slabs = n_state // LANES
    tile = min(S5_TILE, seq)
    sub = min(S5_SUB, tile)
    assert seq % tile == 0 and tile % sub == 0 and bsz == SUBLANES and w_in.shape[-1] % d_s5 == 0
    subs = tile // sub
    pitch = sub + S5_PAD
    u_block = w_in.shape[-1] // d_s5 - 1
    const = lambda *shape: pl.BlockSpec(shape, lambda i: (0,) * len(shape))
    return pl.pallas_call(
        functools.partial(_s5_prompt_kernel, tile=tile, sub=sub, bsz=bsz),
        grid=(seq // tile,),
        in_specs=[
            pl.BlockSpec((bsz, tile, d_model), lambda i: (0, i, 0)),
            pl.BlockSpec((None, d_model, d_s5), lambda i: (layer, 0, u_block)),
        ] + [_layer_block(a, layer, 1) for a in s5p],
        out_specs=[
            pl.BlockSpec((bsz, tile, d_s5), lambda i: (0, i, 0)),
            const(bsz, n_state), const(bsz, n_state),
        ],
        out_shape=[
            jax.ShapeDtypeStruct((bsz, seq, d_s5), BF16),
            jax.ShapeDtypeStruct((bsz, n_state), F32),
            jax.ShapeDtypeStruct((bsz, n_state), F32),
        ],
        scratch_shapes=[
            pltpu.VMEM((subs, slabs, bsz * pitch, LANES), F32),
            pltpu.VMEM((subs, slabs, bsz * pitch, LANES), F32),
            pltpu.VMEM((subs, bsz * sub, d_s5), F32),
            pltpu.VMEM((subs, bsz * pitch, d_s5), F32),
        ],
        compiler_params=pltpu.CompilerParams(
            dimension_semantics=("arbitrary",), vmem_limit_bytes=VMEM_LIMIT),
        name="s5_prompt",
    )(x, w_in, *s5p)


def _mix_sample_kernel(x_ref, w_ref, lb_ref, ng_ref, s_ref, hre_ref, him_ref,
                       bre_ref, bim_ref, c_ref, are_ref, aim_ref, d_ref, wglu_ref, so_all_ref,
                       oa_ref, ob_ref, so_ref, nre_ref, nim_ref, proj_ref, orow_ref, *, heads):
    del so_all_ref
    i = pl.program_id(0)
    d_h = heads * HEAD_DIM
    d_s5 = d_ref.shape[-1]
    slabs = are_ref.shape[-1] // LANES

    @pl.when(i == 0)
    def _():
        proj = _dot(x_ref[...].astype(BF16), w_ref[...])
        proj_ref[...] = proj
        u = proj[:, 4 * d_h:]

        def store(slab, re, im):
            sl = slice(slab * LANES, (slab + 1) * LANES)
            ar, ai = are_ref[:, sl], aim_ref[:, sl]
            hr, hi = hre_ref[:, sl], him_ref[:, sl]
            nre_ref[:, sl] = ar * hr - ai * hi + re
            nim_ref[:, sl] = ar * hi + ai * hr + im

        _s5_input_drive(u.astype(BF16), bre_ref, bim_ref, store)
        load = lambda s: jnp.concatenate(
            [nre_ref[:, s * LANES:(s + 1) * LANES], nim_ref[:, s * LANES:(s + 1) * LANES]],
            axis=-1).astype(BF16)
        y = _s5_readout(load, c_ref, d_s5)
        ob_ref[...] = _s5_output(y, u, d_ref, wglu_ref).astype(ob_ref.dtype)

    r0 = pl.multiple_of(i * SAMPLE_ROWS, SAMPLE_ROWS)
    rows = proj_ref[pl.ds(r0, SAMPLE_ROWS), :]
    q = rows[:, 0:d_h]
    fz = rows[:, d_h:2 * d_h]
    v = rows[:, 2 * d_h:3 * d_h]
    g = rows[:, 3 * d_h:4 * d_h]
    log_f, k = _hgrn_gates(fz, lb_ref[...])
    f = jnp.exp(log_f)
    per = heads * SAMPLE_ROWS
    pieces = [a[:, h * HEAD_DIM:(h + 1) * HEAD_DIM] for a in (f, k) for h in range(heads)]
    assert 2 * per <= HEAD_DIM
    if 2 * per < HEAD_DIM:
        pieces.append(jnp.zeros((HEAD_DIM - 2 * per, HEAD_DIM), F32))
    cols = jnp.concatenate(pieces, axis=0).T
    qb = q.astype(BF16)
    for h in range(heads):
        hl = slice(h * HEAD_DIM, (h + 1) * HEAD_DIM)
        for r in range(SAMPLE_ROWS):
            j = h * SAMPLE_ROWS + r
            fc = cols[:, j:j + 1]
            kc = cols[:, per + j:per + j + 1]
            sn = fc * s_ref[r, h] + kc * v[r:r + 1, hl]
            so_ref[r, h] = sn
            orow_ref[r:r + 1, hl] = _dot(qb[:, hl], sn.astype(BF16))[r:r + 1, :]
    oa_ref[pl.ds(r0, SAMPLE_ROWS), :] = _head_rms_gate(orow_ref[...], g, ng_ref[...]).astype(oa_ref.dtype)


def _mix_sample(x, w_in, layer, lb, ng, s_all, h_re, h_im, s5p, s_new_all):
    n, d_model = x.shape
    d_h = lb.shape[-1]
    heads = d_h // HEAD_DIM
    d_s5 = s5p[5].shape[-1]
    n_state = s5p[3].shape[-1]
    assert n % SAMPLE_ROWS == 0
    const = lambda *shape: pl.BlockSpec(shape, lambda i: (0,) * len(shape))
    state_spec = pl.BlockSpec((None, SAMPLE_ROWS, heads, HEAD_DIM, HEAD_DIM), lambda i: (layer, i, 0, 0, 0))
    args = [x, w_in, lb, ng, s_all, h_re, h_im, *s5p, s_new_all]
    in_specs = [const(n, d_model), _layer_block(w_in, layer, 1), const(1, d_h), const(1, d_h),
                state_spec, _layer_block(h_re, layer, 1), _layer_block(h_im, layer, 1)]
    in_specs += [_layer_block(a, layer, 1) for a in s5p]
    in_specs.append(pl.BlockSpec(memory_space=pl.ANY))
    aliases = {len(args) - 1: 2}
    return pl.pallas_call(
        functools.partial(_mix_sample_kernel, heads=heads),
        grid=(n // SAMPLE_ROWS,),
        in_specs=in_specs,
        out_specs=[
            const(n, d_h), const(n, d_s5), state_spec, const(n, n_state), const(n, n_state),
        ],
        out_shape=[
            jax.ShapeDtypeStruct((n, d_h), BF16),
            jax.ShapeDtypeStruct((n, d_s5), BF16),
            jax.ShapeDtypeStruct(s_all.shape, F32),
            jax.ShapeDtypeStruct((n, n_state), F32),
            jax.ShapeDtypeStruct((n, n_state), F32),
        ],
        scratch_shapes=[
            pltpu.VMEM((n, w_in.shape[-1]), F32),
            pltpu.VMEM((SAMPLE_ROWS, d_h), F32),
        ],
        input_output_aliases=aliases,
        compiler_params=pltpu.CompilerParams(
            dimension_semantics=("arbitrary",), vmem_limit_bytes=VMEM_LIMIT),
        name="mix_sample",
    )(*args)


def _mix_ln1(x, oa, ob, wout_ref, g1_ref, b1_ref, alpha):
    d_a = oa.shape[-1]
    mix = _dot(oa, wout_ref[0:d_a, :]) + _dot(ob, wout_ref[d_a:, :])
    return _layer_norm(alpha * x + mix, g1_ref[...], b1_ref[...])


def _ffn_blocks(x1b, wup_ref, cw_ref, cb_ref, wdown_ref, taps, d_ff):
    acc = None
    for j in range(d_ff // FFN_COLS):
        hs = []
        for base in (0, d_ff):
            cols = slice(base + j * FFN_COLS, base + (j + 1) * FFN_COLS)
            up = _dot(x1b, wup_ref[:, cols])
            m2, m1 = taps(up, cols)
            hs.append(cb_ref[:, cols] + m2 * cw_ref[0:1, cols] + m1 * cw_ref[1:2, cols]
                      + up * cw_ref[2:3, cols])
        val, gate = hs
        hh = (gate * jax.nn.sigmoid(gate)) * val
        part = _dot(hh.astype(BF16), wdown_ref[j * FFN_COLS:(j + 1) * FFN_COLS, :])
        acc = part if acc is None else acc + part
    return acc


def _ffn_prompt_kernel(x_ref, oa_ref, ob_ref, wout_ref, g1_ref, b1_ref, wup_ref, cw_ref, cb_ref,
                       wdown_ref, g2_ref, b2_ref, y_ref, cache_ref,
                       carry_ref, up_ref, hh_ref, acc_ref, x1_ref, x1b_ref, *, tile, alpha, d_ff):
    t = pl.program_id(1)
    lead = SUBLANES
    nb = d_ff // FFN_COLS
    sub = x1_ref.shape[1]
    subs = tile // sub
    slabs = FFN_COLS // LANES
    half_rows = FFN_ROWS // 2

    @pl.when(t == 0)
    def _():
        carry_ref[...] = jnp.zeros_like(carry_ref)

    def mix_ln1(s):
        rows = slice(s * sub, (s + 1) * sub)
        d_a = oa_ref.shape[-1]
        acc_ref[s] = _dot(oa_ref[rows, :], wout_ref[0:d_a, :]) + _dot(ob_ref[rows, :], wout_ref[d_a:, :])
        for r in range(0, sub, FFN_ROWS):
            rr = slice(r, r + FFN_ROWS)
            x1 = _layer_norm(alpha * x_ref[s * sub + r:s * sub + r + FFN_ROWS, :] + acc_ref[s, rr, :],
                             g1_ref[...], b1_ref[...])
            x1_ref[s, rr, :] = x1
            x1b_ref[s, rr, :] = x1.astype(BF16)

    def ln2(s):
        for r in range(0, sub, FFN_ROWS):
            rr = slice(r, r + FFN_ROWS)
            y_ref[s * sub + r:s * sub + r + FFN_ROWS, :] = _layer_norm(
                alpha * x1_ref[s, rr, :] + acc_ref[s, rr, :], g2_ref[...], b2_ref[...])

    def halves(j):
        return (slice(j * FFN_COLS, (j + 1) * FFN_COLS),
                slice(d_ff + j * FFN_COLS, d_ff + (j + 1) * FFN_COLS))

    def up_block(slot, s, j):
        for half, cols in enumerate(halves(j)):
            up = _dot(x1b_ref[s], wup_ref[:, cols])
            for i in range(slabs):
                lanes = slice(cols.start + i * LANES, cols.start + (i + 1) * LANES)
                up_ref[slot, half * slabs + i, lead:lead + sub, :] = up[:, i * LANES:(i + 1) * LANES]
                up_ref[slot, half * slabs + i, lead - 2:lead, :] = carry_ref[:, lanes]

    def conv_block(slot, s, j):
        for i in range(slabs):
            lanes = [slice(c.start + i * LANES, c.start + (i + 1) * LANES) for c in halves(j)]
            rep = lambda a: jnp.broadcast_to(a, (half_rows, LANES))
            prm = [(rep(cw_ref[0:1, l]), rep(cw_ref[1:2, l]), rep(cw_ref[2:3, l]), rep(cb_ref[:, l]))
                   for l in lanes]
            for r in range(0, sub, FFN_ROWS):
                res = []
                for half, (w0, w1, w2, cb) in enumerate(prm):
                    tap = lambda d: up_ref[slot, half * slabs + i, pl.ds(lead + r + d, half_rows, stride=2), :]
                    em, om, e, o = tap(-2), tap(-1), tap(0), tap(1)
                    res.append((cb + em * w0 + om * w1 + e * w2, cb + om * w0 + e * w1 + o * w2))
                (ve, vo), (ge, go) = res
                hh_ref[slot, i, pl.ds(r, half_rows, stride=2), :] = (ge * jax.nn.sigmoid(ge)) * ve
                hh_ref[slot, i, pl.ds(r + 1, half_rows, stride=2), :] = (go * jax.nn.sigmoid(go)) * vo
            for half, l in enumerate(lanes):
                carry_ref[:, l] = up_ref[slot, half * slabs + i, lead + sub - 2:lead + sub, :]

    def down_block(slot, s, j):
        hh = jnp.concatenate([hh_ref[slot, i] for i in range(slabs)], axis=-1).astype(BF16)
        part = _dot(hh, wdown_ref[j * FFN_COLS:(j + 1) * FFN_COLS, :])
        if j == 0:
            acc_ref[s] = part
        else:
            acc_ref[s] += part

    for s in range(subs):
        mix_ln1(s)
    blocks = [(s, j) for s in range(subs) for j in range(nb)]
    up_block(0, *blocks[0])
    for n, (s, j) in enumerate(blocks):
        if n + 1 < len(blocks):
            up_block((n + 1) % 2, *blocks[n + 1])
        conv_block(n % 2, s, j)
        down_block(n % 2, s, j)
        if j == nb - 1:
            ln2(s)
    cache_ref[...] = carry_ref[...]


def _ffn_sample_kernel(x_ref, oa_ref, ob_ref, m2_ref, m1_ref, wout_ref, g1_ref, b1_ref, wup_ref, cw_ref,
                       cb_ref, wdown_ref, g2_ref, b2_ref, up_all_ref, y_ref, up_ref, *, alpha, d_ff):
    del up_all_ref
    x1 = _mix_ln1(x_ref[...], oa_ref[...], ob_ref[...], wout_ref, g1_ref, b1_ref, alpha)

    def taps(up, cols):
        up_ref[:, cols] = up
        return m2_ref[:, cols], m1_ref[:, cols]

    ff = _ffn_blocks(x1.astype(BF16), wup_ref, cw_ref, cb_ref, wdown_ref, taps, d_ff)
    y_ref[...] = _layer_norm(alpha * x1 + ff, g2_ref[...], b2_ref[...])


def _resident(a, layer, grid_rank):
    idx = (layer,) + (0,) * (a.ndim - 1)
    index_map = {1: lambda i: idx, 2: lambda b, t: idx}[grid_rank]
    return pl.BlockSpec((None,) + a.shape[1:], index_map, pipeline_mode=pl.Buffered(1))


def _ffn_prompt(x, oa, ob, fw, layer, alpha):
    bsz, seq, d_model = x.shape
    d_ff = fw[6].shape[1]
    tile = min(FFN_TILE, seq)
    sub = min(FFN_SUB, tile)
    assert seq % tile == 0 and d_ff % FFN_COLS == 0 and tile % sub == 0 and sub % FFN_ROWS == 0
    subs = tile // sub
    tok = lambda d: pl.BlockSpec((None, tile, d), lambda b, t: (b, t, 0))
    return pl.pallas_call(
        functools.partial(_ffn_prompt_kernel, tile=tile, alpha=alpha, d_ff=d_ff),
        grid=(bsz, seq // tile),
        in_specs=[tok(d_model), tok(oa.shape[-1]), tok(ob.shape[-1])] + [_resident(a, layer, 2) for a in fw],
        out_specs=[tok(d_model), pl.BlockSpec((None, 2, 2 * d_ff), lambda b, t: (b, 0, 0))],
        out_shape=[jax.ShapeDtypeStruct(x.shape, F32),
                   jax.ShapeDtypeStruct((bsz, 2, 2 * d_ff), F32)],
        scratch_shapes=[
            pltpu.VMEM((2, 2 * d_ff), F32),
            pltpu.VMEM((2, 2 * FFN_COLS // LANES, SUBLANES + sub, LANES), F32),
            pltpu.VMEM((2, FFN_COLS // LANES, sub, LANES), F32),
            pltpu.VMEM((subs, sub, d_model), F32),
            pltpu.VMEM((subs, sub, d_model), F32),
            pltpu.VMEM((subs, sub, d_model), BF16),
        ],
        compiler_params=pltpu.CompilerParams(
            dimension_semantics=("arbitrary", "arbitrary"), vmem_limit_bytes=VMEM_LIMIT),
        name="ffn_prompt",
    )(x, oa, ob, *fw)


def _ffn_sample(x, oa, ob, tap2_all, tap1_all, fw, layer, alpha, up_all):
    n, d_model = x.shape
    d_ff = fw[6].shape[1]
    whole = lambda a: pl.BlockSpec(a.shape, lambda i: (0,) * a.ndim, pipeline_mode=pl.Buffered(1))
    args = [x, oa, ob, tap2_all, tap1_all, *fw, up_all]
    in_specs = [whole(x), whole(oa), whole(ob), _resident(tap2_all, layer, 1), _resident(tap1_all, layer, 1)]
    in_specs += [_resident(a, layer, 1) for a in fw]
    in_specs.append(pl.BlockSpec(memory_space=pl.ANY))
    aliases = {len(args) - 1: 1}
    return pl.pallas_call(
        functools.partial(_ffn_sample_kernel, alpha=alpha, d_ff=d_ff),
        grid=(1,),
        in_specs=in_specs,
        out_specs=[pl.BlockSpec(x.shape, lambda i: (0, 0)),
                   pl.BlockSpec((None,) + up_all.shape[1:], lambda i: (layer, 0, 0))],
        out_shape=[jax.ShapeDtypeStruct(x.shape, F32), jax.ShapeDtypeStruct(up_all.shape, F32)],
        input_output_aliases=aliases,
        compiler_params=pltpu.CompilerParams(
            dimension_semantics=("arbitrary",), vmem_limit_bytes=VMEM_LIMIT),
        name="ffn_sample",
    )(*args)


def _s5_params(lam_re, lam_im, log_dt, b_re, b_im, c_re, c_im, d_skip, w_glu):
    groups, n_p = lam_re.shape
    lr = jnp.minimum(lam_re, -1e-4)
    li = lam_im
    dt = jnp.exp(log_dt)[:, None]
    mag = jnp.exp(lr * dt)
    ab_re = mag * jnp.cos(li * dt)
    ab_im = mag * jnp.sin(li * dt)
    den = lr * lr + li * li
    nr = ab_re - 1.0
    coef_re = (nr * lr + ab_im * li) / den
    coef_im = (ab_im * lr - nr * li) / den
    bb_re = coef_re[..., None] * b_re - coef_im[..., None] * b_im
    bb_im = coef_re[..., None] * b_im + coef_im[..., None] * b_re

    gpb = LANES // S5_GROUP_CH
    kblocks = groups // gpb
    eye = jnp.eye(gpb, dtype=F32)

    def b_blocks(bb):
        bb = bb.reshape(kblocks, gpb, n_p, S5_GROUP_CH)
        return jnp.einsum('mgpj,gh->mgjhp', bb, eye).reshape(kblocks, LANES, gpb * n_p).astype(BF16)

    gps = LANES // n_p
    slabs = groups // gps
    gpt = (2 * LANES) // S5_GROUP_CH
    pos = (jnp.arange(slabs)[:, None] * gps + jnp.arange(gps)[None, :]) % gpt
    onehot = jax.nn.one_hot(pos, gpt, dtype=F32)

    def c_half(c):
        c = c.reshape(slabs, gps, S5_GROUP_CH, n_p)
        return jnp.einsum('sgjp,sgh->sgphj', c, onehot).reshape(slabs, gps * n_p, gpt * S5_GROUP_CH)

    cblk = jnp.concatenate([c_half(c_re), -c_half(c_im)], axis=1).astype(BF16)
    return (b_blocks(bb_re), b_blocks(bb_im), cblk,
            ab_re.reshape(1, groups * n_p), ab_im.reshape(1, groups * n_p),
            d_skip.reshape(1, -1), w_glu.astype(BF16))


def kernel(x_prompt, x_sample, state_hgrn, state_s5_re, state_s5_im, cache_ffn_conv, w_in, hgrn_lb_logits, hgrn_norm_g, s5_lambda_re, s5_lambda_im, s5_log_dt, s5_b_re, s5_b_im, s5_c_re, s5_c_im, s5_d, w_glu, w_out, ln1_g, ln1_b, w_ffn_up, ffn_conv_w, ffn_conv_b, w_ffn_down, ln2_g, ln2_b):
    depth = w_in.shape[0]
    d_h = hgrn_lb_logits.shape[-1]
    heads = d_h // HEAD_DIM
    n_dec = x_sample.shape[0]
    groups, n_p = s5_lambda_re.shape[1:]
    alpha = (2 * depth) ** 0.25

    sm = jax.nn.softmax(hgrn_lb_logits.astype(F32), axis=0)
    lower_bounds = jnp.cumsum(sm, axis=0) - sm[0:1]

    w_in_b = w_in.astype(BF16)
    s5p = jax.vmap(_s5_params)(s5_lambda_re, s5_lambda_im, s5_log_dt, s5_b_re, s5_b_im,
                               s5_c_re, s5_c_im, s5_d, w_glu)
    rows = lambda a: a.reshape(depth, 1, -1)
    fw = (w_out.astype(BF16), rows(ln1_g), rows(ln1_b), w_ffn_up.astype(BF16),
          ffn_conv_w, rows(ffn_conv_b), w_ffn_down.astype(BF16), rows(ln2_g), rows(ln2_b))
    h_re = state_s5_re.reshape(depth, n_dec, groups * n_p)
    h_im = state_s5_im.reshape(depth, n_dec, groups * n_p)
    tap2_all = cache_ffn_conv[:, :, 0, :]
    tap1_all = cache_ffn_conv[:, :, 1, :]

    xp = x_prompt
    xs = x_sample.reshape(n_dec, -1)
    outs = {k: [] for k in ("hp", "rp", "ip", "cp", "rs", "is")}
    hs_new = jnp.zeros(state_hgrn.shape, F32)
    up_new = jnp.zeros(tap1_all.shape, F32)
    for l in range(depth):
        lb = lower_bounds[l].reshape(1, d_h)
        ng = jnp.tile(hgrn_norm_g[l], heads).reshape(1, d_h)

        oa, s_new = _hgrn_prompt(xp, w_in_b, l, lb, ng)
        ob, re_new, im_new = _s5_prompt(xp, w_in_b, l, s5p)
        xp, cache_new = _ffn_prompt(xp, oa, ob, fw, l, alpha)
        outs["hp"].append(s_new)
        outs["rp"].append(re_new.reshape(-1, groups, n_p))
        outs["ip"].append(im_new.reshape(-1, groups, n_p))
        outs["cp"].append(cache_new)

        oa, ob, hs_new, re_new, im_new = _mix_sample(xs, w_in_b, l, lb, ng, state_hgrn, h_re, h_im, s5p, hs_new)
        xs, up_new = _ffn_sample(xs, oa, ob, tap2_all, tap1_all, fw, l, alpha, up_new)
        outs["rs"].append(re_new.reshape(n_dec, groups, n_p))
        outs["is"].append(im_new.reshape(n_dec, groups, n_p))

    st = {k: jnp.stack(v) for k, v in outs.items()}
    cs_new = jnp.stack([tap1_all, up_new], axis=2)
    return (xp, xs.reshape(x_sample.shape), st["hp"], st["rp"], st["ip"], st["cp"],
            hs_new, st["rs"], st["is"], cs_new)
```

```python
import functools
import math

import jax
import jax.numpy as jnp
import numpy as np
from jax import lax
from jax.experimental import pallas as pl
from jax.experimental.pallas import tpu as pltpu

F32 = jnp.float32
BF16 = jnp.bfloat16

HEAD_DIM = 128
CHUNK = 64
S5_GROUP_CH = 16
LANES = 128
SUBLANES = 8
MXU_COLS = 256
LN_EPS = 1e-5
RMS_EPS = 1e-6
F_FLOOR = 1e-20
SAFE_SPAN = 60.0
VMEM_LIMIT = 56 * 1024 * 1024

HGRN_TILE = 1024
HGRN_SUB = 256
S5_TILE = 128
S5_SUB = 64
S5_PAD = 4
FFN_TILE = 512
FFN_SUB = 256
FFN_COLS = 256
FFN_ROWS = 32
SAMPLE_ROWS = 16


def _dot(a, b):
    return jnp.dot(a, b, preferred_element_type=F32)


def _dot_nt(a, b):
    return lax.dot_general(a, b, (((1,), (1,)), ((), ())), preferred_element_type=F32)


def _dot_tn(a, b):
    return lax.dot_general(a, b, (((0,), (0,)), ((), ())), preferred_element_type=F32)


def _split3(x):
    hi = x.astype(BF16)
    r1 = x - hi.astype(F32)
    mid = r1.astype(BF16)
    lo = (r1 - mid.astype(F32)).astype(BF16)
    return hi, mid, lo


def _dot3(m, parts):
    return _dot(m, parts[0]) + _dot(m, parts[1]) + _dot(m, parts[2])


def _cumsum_rows(x):
    n, w = x.shape
    tiles = n // SUBLANES
    x3 = x.reshape(tiles, SUBLANES, w)
    sub = lax.broadcasted_iota(jnp.int32, (1, SUBLANES, 1), 1)
    s = 1
    while s < SUBLANES:
        x3 = x3 + jnp.where(sub >= s, pltpu.roll(x3, s, axis=1), 0.0)
        s *= 2
    outs = [x3[0]]
    for i in range(1, tiles):
        outs.append(x3[i] + outs[-1][SUBLANES - 1:SUBLANES, :])
    return jnp.concatenate(outs, axis=0)


def _layer_norm(x, g, b):
    mu = jnp.mean(x, axis=-1, keepdims=True)
    xc = x - mu
    var = jnp.mean(xc * xc, axis=-1, keepdims=True)
    return xc * lax.rsqrt(var + LN_EPS) * g + b


def _hgrn_gates(fz, lb):
    oml = 1.0 - lb
    f = lb + oml * jax.nn.sigmoid(fz)
    log_f = jnp.log(jnp.maximum(f, F_FLOOR))
    k = oml * jax.nn.sigmoid(-fz)
    return log_f, k


def _head_rms_gate(o, g, ng):
    heads = o.shape[-1] // HEAD_DIM
    outs = []
    for h in range(heads):
        oh = o[:, h * HEAD_DIM:(h + 1) * HEAD_DIM]
        ms = jnp.mean(oh * oh, axis=-1, keepdims=True)
        outs.append(oh * lax.rsqrt(ms + RMS_EPS))
    return jnp.concatenate(outs, axis=-1) * ng * jax.nn.sigmoid(g)


def _hgrn_prompt_kernel(x_ref, w_ref, lb_ref, ng_ref, lvl_ref, oa_ref, s_ref,
                        st_ref, st0_ref, proj_ref, xb_ref, *, tile, heads):
    t = pl.program_id(1)
    d_h = heads * HEAD_DIM

    @pl.when(t == 0)
    def _():
        st_ref[...] = jnp.zeros_like(st_ref)

    sub = min(HGRN_SUB, tile)

    pieces = sub // CHUNK
    piece_cols = 4 * d_h // pieces

    def project(k, j):
        rows = slice(k * sub, (k + 1) * sub)
        cols = slice(j * piece_cols, (j + 1) * piece_cols)
        if j == 0:
            xb_ref[rows, :] = x_ref[rows, :].astype(BF16)
        proj_ref[rows, cols] = _dot(xb_ref[rows, :], w_ref[:, cols])

    st0_ref[...] = st_ref[...]
    lb = lb_ref[...]
    ng = ng_ref[...]
    ri = lax.broadcasted_iota(jnp.int32, (CHUNK, CHUNK), 0)
    ci = lax.broadcasted_iota(jnp.int32, (CHUNK, CHUNK), 1)
    row = lax.broadcasted_iota(jnp.int32, (CHUNK, 1), 0)
    mid = CHUNK // 2 - 1

    def anchored_scores(q, k, b, log_f):
        bm = b[mid:mid + 1, :]
        qe = (q * jnp.exp(jnp.minimum(b - bm, SAFE_SPAN))).astype(BF16)
        ke = (k * jnp.exp(jnp.minimum(bm - b, SAFE_SPAN))).astype(BF16)
        return [jnp.where(ri >= ci, _dot_nt(qe[:, h * HEAD_DIM:(h + 1) * HEAD_DIM],
                                             ke[:, h * HEAD_DIM:(h + 1) * HEAD_DIM]), 0.0)
                for h in range(heads)]

    def hierarchical_scores(q, k, b, log_f):
        anchors = _dot3(lvl_ref[...], _split3(log_f))
        qb = q.astype(BF16)
        kb = k.astype(BF16)
        sc = [jnp.where(ri == ci, _dot_nt(qb[:, h * HEAD_DIM:(h + 1) * HEAD_DIM],
                                          kb[:, h * HEAD_DIM:(h + 1) * HEAD_DIM]), 0.0)
              for h in range(heads)]
        lvl = 0
        hs = CHUNK // 2
        while hs >= 1:
            a = anchors[lvl * CHUNK:(lvl + 1) * CHUNK, :]
            upper = (row & (2 * hs - 1)) >= hs
            qe = (q * jnp.where(upper, jnp.exp(jnp.minimum(b - a, 0.0)), 0.0)).astype(BF16)
            ke = (k * jnp.where(upper, 0.0, jnp.exp(jnp.minimum(a - b, 0.0)))).astype(BF16)
            shift = int(math.log2(2 * hs))
            same = (ri >> shift) == (ci >> shift)
            for h in range(heads):
                sl = slice(h * HEAD_DIM, (h + 1) * HEAD_DIM)
                sc[h] = sc[h] + jnp.where(same, _dot_nt(qe[:, sl], ke[:, sl]), 0.0)
            lvl += 1
            hs //= 2
        return sc

    def chunk(rows, scores_fn):
        q = proj_ref[rows, 0:d_h]
        fz = proj_ref[rows, d_h:2 * d_h]
        v = proj_ref[rows, 2 * d_h:3 * d_h]
        g = proj_ref[rows, 3 * d_h:4 * d_h]
        log_f, k = _hgrn_gates(fz, lb)
        b = _cumsum_rows(log_f)
        bm = b[mid:mid + 1, :]
        bl = b[CHUNK - 1:CHUNK, :]
        sc = scores_fn(q, k, b, log_f)
        qs = (q * jnp.exp(b)).astype(BF16)
        kl = (k * jnp.exp(bl - b)).astype(BF16)
        vb = v.astype(BF16)
        dec = jnp.exp(bl)
        outs = []
        for h in range(heads):
            sl = slice(h * HEAD_DIM, (h + 1) * HEAD_DIM)
            st = st_ref[h]
            outs.append(_dot(sc[h].astype(BF16), vb[:, sl]) + _dot_nt(qs[:, sl], st.astype(BF16)))
            st_ref[h] = st * dec[:, sl] + _dot_tn(vb[:, sl], kl[:, sl])
        o = jnp.concatenate(outs, axis=-1)
        oa_ref[rows, :] = _head_rms_gate(o, g, ng).astype(oa_ref.dtype)
        return jnp.minimum(bm, bl - bm)

    span = None
    for j in range(pieces):
        project(0, j)
    for c in range(tile // CHUNK):
        k, j = divmod(c, pieces)
        if (k + 1) * sub < tile:
            project(k + 1, j)
        worst = chunk(slice(c * CHUNK, (c + 1) * CHUNK), anchored_scores)
        span = worst if span is None else jnp.minimum(span, worst)
    span_ok = jnp.min(span) >= -SAFE_SPAN

    @pl.when(jnp.logical_not(span_ok))
    def _():
        st_ref[...] = st0_ref[...]

        def redo(c, carry):
            chunk(pl.ds(pl.multiple_of(c * CHUNK, CHUNK), CHUNK), hierarchical_scores)
            return carry

        lax.fori_loop(0, tile // CHUNK, redo, 0)

    @pl.when(t == pl.num_programs(1) - 1)
    def _():
        for h in range(heads):
            s_ref[h] = st_ref[h].T


def _anchor_matrices():
    r = np.arange(CHUNK)
    lvls = []
    hs = CHUNK // 2
    while hs >= 1:
        anchor = (r // (2 * hs)) * (2 * hs) + hs - 1
        lvls.append((r[None, :] <= anchor[:, None]).astype(np.float32))
        hs //= 2
    return jnp.asarray(np.concatenate(lvls, axis=0), BF16)


def _hgrn_prompt(x, w_in, layer, lb, ng):
    bsz, seq, d_model = x.shape
    d_h = lb.shape[-1]
    heads = d_h // HEAD_DIM
    tile = min(HGRN_TILE, seq)
    assert seq % tile == 0 and tile % CHUNK == 0
    lvl = _anchor_matrices()
    const = lambda *shape: pl.BlockSpec(shape, lambda b, t: (0,) * len(shape))
    return pl.pallas_call(
        functools.partial(_hgrn_prompt_kernel, tile=tile, heads=heads),
        grid=(bsz, seq // tile),
        in_specs=[
            pl.BlockSpec((None, tile, d_model), lambda b, t: (b, t, 0)),
            pl.BlockSpec((None, d_model, 4 * d_h), lambda b, t: (layer, 0, 0)),
            const(1, d_h),
            const(1, d_h),
            const(lvl.shape[0], CHUNK),
        ],
        out_specs=[
            pl.BlockSpec((None, tile, d_h), lambda b, t: (b, t, 0)),
            pl.BlockSpec((None, heads, HEAD_DIM, HEAD_DIM), lambda b, t: (b, 0, 0, 0)),
        ],
        out_shape=[
            jax.ShapeDtypeStruct((bsz, seq, d_h), BF16),
            jax.ShapeDtypeStruct((bsz, heads, HEAD_DIM, HEAD_DIM), F32),
        ],
        scratch_shapes=[
            pltpu.VMEM((heads, HEAD_DIM, HEAD_DIM), F32),
            pltpu.VMEM((heads, HEAD_DIM, HEAD_DIM), F32),
            pltpu.VMEM((tile, 4 * d_h), F32),
            pltpu.VMEM((tile, d_model), BF16),
        ],
        compiler_params=pltpu.CompilerParams(
            dimension_semantics=("arbitrary", "arbitrary"), vmem_limit_bytes=VMEM_LIMIT),
        name="hgrn_prompt",
    )(x, w_in, lb, ng, lvl)


def _s5_input_drive(ub, bre_ref, bim_ref, store):
    kblocks = bre_ref.shape[0]
    slabs_per_block = bre_ref.shape[2] // LANES
    for m in range(kblocks):
        um = ub[:, m * LANES:(m + 1) * LANES]
        re = _dot(um, bre_ref[m])
        im = _dot(um, bim_ref[m])
        for s in range(slabs_per_block):
            store(m * slabs_per_block + s, re[:, s * LANES:(s + 1) * LANES], im[:, s * LANES:(s + 1) * LANES])


def _s5_readout(load, c_ref, d_s5):
    slabs = c_ref.shape[0]
    tiles = d_s5 // MXU_COLS
    per_tile = slabs // tiles
    ys = []
    for n in range(tiles):
        acc = None
        for s in range(n * per_tile, (n + 1) * per_tile):
            part = _dot(load(s), c_ref[s])
            acc = part if acc is None else acc + part
        ys.append(acc)
    return jnp.concatenate(ys, axis=-1)


def _s5_output(y, u, d_ref, wglu_ref):
    y = y + d_ref[...] * u
    y = 0.5 * y * (1.0 + lax.erf(y * math.sqrt(0.5)))
    return y * jax.nn.sigmoid(_dot(y.astype(BF16), wglu_ref[...]))


def _s5_prompt_kernel(x_ref, wu_ref, bre_ref, bim_ref, c_ref, are_ref, aim_ref, d_ref, wglu_ref,
                      ob_ref, hre_ref, him_ref, xr_ref, xi_ref, u_ref, y_ref, *, tile, sub, bsz):
    i = pl.program_id(0)
    pitch = sub + S5_PAD
    subs = tile // sub
    slabs = xr_ref.shape[1]
    d_s5 = d_ref.shape[-1]

    @pl.when(i == 0)
    def _():
        hre_ref[...] = jnp.zeros_like(hre_ref)
        him_ref[...] = jnp.zeros_like(him_ref)
        xr_ref[...] = jnp.zeros_like(xr_ref)
        xi_ref[...] = jnp.zeros_like(xi_ref)

    def drive(k):
        x = x_ref[:, k * sub:(k + 1) * sub, :].reshape(bsz * sub, x_ref.shape[-1])
        u = _dot(x.astype(BF16), wu_ref[...])
        u_ref[k] = u

        def store(slab, re, im):
            for b in range(bsz):
                xr_ref[k, slab, b * pitch:b * pitch + sub, :] = re[b * sub:(b + 1) * sub, :]
                xi_ref[k, slab, b * pitch:b * pitch + sub, :] = im[b * sub:(b + 1) * sub, :]

        _s5_input_drive(u.astype(BF16), bre_ref, bim_ref, store)

    def scan(k, hr, hi):
        for tt in range(sub):
            rows = pl.ds(tt, bsz, stride=pitch)
            for s in range(slabs):
                ar = are_ref[:, s * LANES:(s + 1) * LANES]
                ai = aim_ref[:, s * LANES:(s + 1) * LANES]
                nr = ar * hr[s] - ai * hi[s] + xr_ref[k, s, rows, :]
                ni = ar * hi[s] + ai * hr[s] + xi_ref[k, s, rows, :]
                xr_ref[k, s, rows, :] = nr
                xi_ref[k, s, rows, :] = ni
                hr[s], hi[s] = nr, ni
        return hr, hi

    def readout(k):
        load = lambda s: jnp.concatenate([xr_ref[k, s], xi_ref[k, s]], axis=-1).astype(BF16)
        y_ref[k] = _s5_readout(load, c_ref, d_s5)
        y = jnp.concatenate([y_ref[k, b * pitch:b * pitch + sub, :] for b in range(bsz)], axis=0)
        o = _s5_output(y, u_ref[k], d_ref, wglu_ref)
        ob_ref[:, k * sub:(k + 1) * sub, :] = o.reshape(bsz, sub, d_s5).astype(ob_ref.dtype)

    for k in range(subs):
        drive(k)
    hr = [hre_ref[:, s * LANES:(s + 1) * LANES] for s in range(slabs)]
    hi = [him_ref[:, s * LANES:(s + 1) * LANES] for s in range(slabs)]
    for k in range(subs):
        hr, hi = scan(k, hr, hi)
        readout(k)
    for s in range(slabs):
        hre_ref[:, s * LANES:(s + 1) * LANES] = hr[s]
        him_ref[:, s * LANES:(s + 1) * LANES] = hi[s]


def _layer_block(a, layer, grid_rank):
    idx = (layer,) + (0,) * (a.ndim - 1)
    index_map = {1: lambda i: idx, 2: lambda b, t: idx}[grid_rank]
    return pl.BlockSpec((None,) + a.shape[1:], index_map)


def _s5_prompt(x, w_in, layer, s5p):
    bsz, seq, d_model = x.shape
    d_s5 = s5p[5].shape[-1]
    n_state = s5p[3].shape[-1]
    slabs = n_state // LANES
    tile = min(S5_TILE, seq)
    sub = min(S5_SUB, tile)
    assert seq % tile == 0 and tile % sub == 0 and bsz == SUBLANES and w_in.shape[-1] % d_s5 == 0
    subs = tile // sub
    pitch = sub + S5_PAD
    u_block = w_in.shape[-1] // d_s5 - 1
    const = lambda *shape: pl.BlockSpec(shape, lambda i: (0,) * len(shape))
    return pl.pallas_call(
        functools.partial(_s5_prompt_kernel, tile=tile, sub=sub, bsz=bsz),
        grid=(seq // tile,),
        in_specs=[
            pl.BlockSpec((bsz, tile, d_model), lambda i: (0, i, 0)),
            pl.BlockSpec((None, d_model, d_s5), lambda i: (layer, 0, u_block)),
        ] + [_layer_block(a, layer, 1) for a in s5p],
        out_specs=[
            pl.BlockSpec((bsz, tile, d_s5), lambda i: (0, i, 0)),
            const(bsz, n_state), const(bsz, n_state),
        ],
        out_shape=[
            jax.ShapeDtypeStruct((bsz, seq, d_s5), BF16),
            jax.ShapeDtypeStruct((bsz, n_state), F32),
            jax.ShapeDtypeStruct((bsz, n_state), F32),
        ],
        scratch_shapes=[
            pltpu.VMEM((subs, slabs, bsz * pitch, LANES), F32),
            pltpu.VMEM((subs, slabs, bsz * pitch, LANES), F32),
            pltpu.VMEM((subs, bsz * sub, d_s5), F32),
            pltpu.VMEM((subs, bsz * pitch, d_s5), F32),
        ],
        compiler_params=pltpu.CompilerParams(
            dimension_semantics=("arbitrary",), vmem_limit_bytes=VMEM_LIMIT),
        name="s5_prompt",
    )(x, w_in, *s5p)


def _mix_sample_kernel(x_ref, w_ref, lb_ref, ng_ref, s_ref, hre_ref, him_ref,
                       bre_ref, bim_ref, c_ref, are_ref, aim_ref, d_ref, wglu_ref, so_all_ref,
                       oa_ref, ob_ref, so_ref, nre_ref, nim_ref, proj_ref, orow_ref, *, heads):
    del so_all_ref
    i = pl.program_id(0)
    d_h = heads * HEAD_DIM
    d_s5 = d_ref.shape[-1]
    slabs = are_ref.shape[-1] // LANES

    @pl.when(i == 0)
    def _():
        proj = _dot(x_ref[...].astype(BF16), w_ref[...])
        proj_ref[...] = proj
        u = proj[:, 4 * d_h:]

        def store(slab, re, im):
            sl = slice(slab * LANES, (slab + 1) * LANES)
            ar, ai = are_ref[:, sl], aim_ref[:, sl]
            hr, hi = hre_ref[:, sl], him_ref[:, sl]
            nre_ref[:, sl] = ar * hr - ai * hi + re
            nim_ref[:, sl] = ar * hi + ai * hr + im

        _s5_input_drive(u.astype(BF16), bre_ref, bim_ref, store)
        load = lambda s: jnp.concatenate(
            [nre_ref[:, s * LANES:(s + 1) * LANES], nim_ref[:, s * LANES:(s + 1) * LANES]],
            axis=-1).astype(BF16)
        y = _s5_readout(load, c_ref, d_s5)
        ob_ref[...] = _s5_output(y, u, d_ref, wglu_ref).astype(ob_ref.dtype)

    r0 = pl.multiple_of(i * SAMPLE_ROWS, SAMPLE_ROWS)
    rows = proj_ref[pl.ds(r0, SAMPLE_ROWS), :]
    q = rows[:, 0:d_h]
    fz = rows[:, d_h:2 * d_h]
    v = rows[:, 2 * d_h:3 * d_h]
    g = rows[:, 3 * d_h:4 * d_h]
    log_f, k = _hgrn_gates(fz, lb_ref[...])
    f = jnp.exp(log_f)
    per = heads * SAMPLE_ROWS
    pieces = [a[:, h * HEAD_DIM:(h + 1) * HEAD_DIM] for a in (f, k) for h in range(heads)]
    assert 2 * per <= HEAD_DIM
    if 2 * per < HEAD_DIM:
        pieces.append(jnp.zeros((HEAD_DIM - 2 * per, HEAD_DIM), F32))
    cols = jnp.concatenate(pieces, axis=0).T
    qb = q.astype(BF16)
    for h in range(heads):
        hl = slice(h * HEAD_DIM, (h + 1) * HEAD_DIM)
        for r in range(SAMPLE_ROWS):
            j = h * SAMPLE_ROWS + r
            fc = cols[:, j:j + 1]
            kc = cols[:, per + j:per + j + 1]
            sn = fc * s_ref[r, h] + kc * v[r:r + 1, hl]
            so_ref[r, h] = sn
            orow_ref[r:r + 1, hl] = _dot(qb[:, hl], sn.astype(BF16))[r:r + 1, :]
    oa_ref[pl.ds(r0, SAMPLE_ROWS), :] = _head_rms_gate(orow_ref[...], g, ng_ref[...]).astype(oa_ref.dtype)


def _mix_sample(x, w_in, layer, lb, ng, s_all, h_re, h_im, s5p, s_new_all):
    n, d_model = x.shape
    d_h = lb.shape[-1]
    heads = d_h // HEAD_DIM
    d_s5 = s5p[5].shape[-1]
    n_state = s5p[3].shape[-1]
    assert n % SAMPLE_ROWS == 0
    const = lambda *shape: pl.BlockSpec(shape, lambda i: (0,) * len(shape))
    state_spec = pl.BlockSpec((None, SAMPLE_ROWS, heads, HEAD_DIM, HEAD_DIM), lambda i: (layer, i, 0, 0, 0))
    args = [x, w_in, lb, ng, s_all, h_re, h_im, *s5p, s_new_all]
    in_specs = [const(n, d_model), _layer_block(w_in, layer, 1), const(1, d_h), const(1, d_h),
                state_spec, _layer_block(h_re, layer, 1), _layer_block(h_im, layer, 1)]
    in_specs += [_layer_block(a, layer, 1) for a in s5p]
    in_specs.append(pl.BlockSpec(memory_space=pl.ANY))
    aliases = {len(args) - 1: 2}
    return pl.pallas_call(
        functools.partial(_mix_sample_kernel, heads=heads),
        grid=(n // SAMPLE_ROWS,),
        in_specs=in_specs,
        out_specs=[
            const(n, d_h), const(n, d_s5), state_spec, const(n, n_state), const(n, n_state),
        ],
        out_shape=[
            jax.ShapeDtypeStruct((n, d_h), BF16),
            jax.ShapeDtypeStruct((n, d_s5), BF16),
            jax.ShapeDtypeStruct(s_all.shape, F32),
            jax.ShapeDtypeStruct((n, n_state), F32),
            jax.ShapeDtypeStruct((n, n_state), F32),
        ],
        scratch_shapes=[
            pltpu.VMEM((n, w_in.shape[-1]), F32),
            pltpu.VMEM((SAMPLE_ROWS, d_h), F32),
        ],
        input_output_aliases=aliases,
        compiler_params=pltpu.CompilerParams(
            dimension_semantics=("arbitrary",), vmem_limit_bytes=VMEM_LIMIT),
        name="mix_sample",
    )(*args)


def _mix_ln1(x, oa, ob, wout_ref, g1_ref, b1_ref, alpha):
    d_a = oa.shape[-1]
    mix = _dot(oa, wout_ref[0:d_a, :]) + _dot(ob, wout_ref[d_a:, :])
    return _layer_norm(alpha * x + mix, g1_ref[...], b1_ref[...])


def _ffn_blocks(x1b, wup_ref, cw_ref, cb_ref, wdown_ref, taps, d_ff):
    acc = None
    for j in range(d_ff // FFN_COLS):
        hs = []
        for base in (0, d_ff):
            cols = slice(base + j * FFN_COLS, base + (j + 1) * FFN_COLS)
            up = _dot(x1b, wup_ref[:, cols])
            m2, m1 = taps(up, cols)
            hs.append(cb_ref[:, cols] + m2 * cw_ref[0:1, cols] + m1 * cw_ref[1:2, cols]
                      + up * cw_ref[2:3, cols])
        val, gate = hs
        hh = (gate * jax.nn.sigmoid(gate)) * val
        part = _dot(hh.astype(BF16), wdown_ref[j * FFN_COLS:(j + 1) * FFN_COLS, :])
        acc = part if acc is None else acc + part
    return acc


def _ffn_prompt_kernel(x_ref, oa_ref, ob_ref, wout_ref, g1_ref, b1_ref, wup_ref, cw_ref, cb_ref,
                       wdown_ref, g2_ref, b2_ref, y_ref, cache_ref,
                       carry_ref, up_ref, hh_ref, acc_ref, x1_ref, x1b_ref, *, tile, alpha, d_ff):
    t = pl.program_id(1)
    lead = SUBLANES
    nb = d_ff // FFN_COLS
    sub = x1_ref.shape[1]
    subs = tile // sub
    slabs = FFN_COLS // LANES
    half_rows = FFN_ROWS // 2

    @pl.when(t == 0)
    def _():
        carry_ref[...] = jnp.zeros_like(carry_ref)

    def mix_ln1(s):
        rows = slice(s * sub, (s + 1) * sub)
        d_a = oa_ref.shape[-1]
        acc_ref[s] = _dot(oa_ref[rows, :], wout_ref[0:d_a, :]) + _dot(ob_ref[rows, :], wout_ref[d_a:, :])
        for r in range(0, sub, FFN_ROWS):
            rr = slice(r, r + FFN_ROWS)
            x1 = _layer_norm(alpha * x_ref[s * sub + r:s * sub + r + FFN_ROWS, :] + acc_ref[s, rr, :],
                             g1_ref[...], b1_ref[...])
            x1_ref[s, rr, :] = x1
            x1b_ref[s, rr, :] = x1.astype(BF16)

    def ln2(s):
        for r in range(0, sub, FFN_ROWS):
            rr = slice(r, r + FFN_ROWS)
            y_ref[s * sub + r:s * sub + r + FFN_ROWS, :] = _layer_norm(
                alpha * x1_ref[s, rr, :] + acc_ref[s, rr, :], g2_ref[...], b2_ref[...])

    def halves(j):
        return (slice(j * FFN_COLS, (j + 1) * FFN_COLS),
                slice(d_ff + j * FFN_COLS, d_ff + (j + 1) * FFN_COLS))

    def up_block(slot, s, j):
        for half, cols in enumerate(halves(j)):
            up = _dot(x1b_ref[s], wup_ref[:, cols])
            for i in range(slabs):
                lanes = slice(cols.start + i * LANES, cols.start + (i + 1) * LANES)
                up_ref[slot, half * slabs + i, lead:lead + sub, :] = up[:, i * LANES:(i + 1) * LANES]
                up_ref[slot, half * slabs + i, lead - 2:lead, :] = carry_ref[:, lanes]

    def conv_block(slot, s, j):
        for i in range(slabs):
            lanes = [slice(c.start + i * LANES, c.start + (i + 1) * LANES) for c in halves(j)]
            rep = lambda a: jnp.broadcast_to(a, (half_rows, LANES))
            prm = [(rep(cw_ref[0:1, l]), rep(cw_ref[1:2, l]), rep(cw_ref[2:3, l]), rep(cb_ref[:, l]))
                   for l in lanes]
            for r in range(0, sub, FFN_ROWS):
                res = []
                for half, (w0, w1, w2, cb) in enumerate(prm):
                    tap = lambda d: up_ref[slot, half * slabs + i, pl.ds(lead + r + d, half_rows, stride=2), :]
                    em, om, e, o = tap(-2), tap(-1), tap(0), tap(1)
                    res.append((cb + em * w0 + om * w1 + e * w2, cb + om * w0 + e * w1 + o * w2))
                (ve, vo), (ge, go) = res
                hh_ref[slot, i, pl.ds(r, half_rows, stride=2), :] = (ge * jax.nn.sigmoid(ge)) * ve
                hh_ref[slot, i, pl.ds(r + 1, half_rows, stride=2), :] = (go * jax.nn.sigmoid(go)) * vo
            for half, l in enumerate(lanes):
                carry_ref[:, l] = up_ref[slot, half * slabs + i, lead + sub - 2:lead + sub, :]

    def down_block(slot, s, j):
        hh = jnp.concatenate([hh_ref[slot, i] for i in range(slabs)], axis=-1).astype(BF16)
        part = _dot(hh, wdown_ref[j * FFN_COLS:(j + 1) * FFN_COLS, :])
        if j == 0:
            acc_ref[s] = part
        else:
            acc_ref[s] += part

    for s in range(subs):
        mix_ln1(s)
    blocks = [(s, j) for s in range(subs) for j in range(nb)]
    up_block(0, *blocks[0])
    for n, (s, j) in enumerate(blocks):
        if n + 1 < len(blocks):
            up_block((n + 1) % 2, *blocks[n + 1])
        conv_block(n % 2, s, j)
        down_block(n % 2, s, j)
        if j == nb - 1:
            ln2(s)
    cache_ref[...] = carry_ref[...]


def _ffn_sample_kernel(x_ref, oa_ref, ob_ref, m2_ref, m1_ref, wout_ref, g1_ref, b1_ref, wup_ref, cw_ref,
                       cb_ref, wdown_ref, g2_ref, b2_ref, up_all_ref, y_ref, up_ref, *, alpha, d_ff):
    del up_all_ref
    x1 = _mix_ln1(x_ref[...], oa_ref[...], ob_ref[...], wout_ref, g1_ref, b1_ref, alpha)

    def taps(up, cols):
        up_ref[:, cols] = up
        return m2_ref[:, cols], m1_ref[:, cols]

    ff = _ffn_blocks(x1.astype(BF16), wup_ref, cw_ref, cb_ref, wdown_ref, taps, d_ff)
    y_ref[...] = _layer_norm(alpha * x1 + ff, g2_ref[...], b2_ref[...])


def _resident(a, layer, grid_rank):
    idx = (layer,) + (0,) * (a.ndim - 1)
    index_map = {1: lambda i: idx, 2: lambda b, t: idx}[grid_rank]
    return pl.BlockSpec((None,) + a.shape[1:], index_map, pipeline_mode=pl.Buffered(1))


def _ffn_prompt(x, oa, ob, fw, layer, alpha):
    bsz, seq, d_model = x.shape
    d_ff = fw[6].shape[1]
    tile = min(FFN_TILE, seq)
    sub = min(FFN_SUB, tile)
    assert seq % tile == 0 and d_ff % FFN_COLS == 0 and tile % sub == 0 and sub % FFN_ROWS == 0
    subs = tile // sub
    tok = lambda d: pl.BlockSpec((None, tile, d), lambda b, t: (b, t, 0))
    return pl.pallas_call(
        functools.partial(_ffn_prompt_kernel, tile=tile, alpha=alpha, d_ff=d_ff),
        grid=(bsz, seq // tile),
        in_specs=[tok(d_model), tok(oa.shape[-1]), tok(ob.shape[-1])] + [_resident(a, layer, 2) for a in fw],
        out_specs=[tok(d_model), pl.BlockSpec((None, 2, 2 * d_ff), lambda b, t: (b, 0, 0))],
        out_shape=[jax.ShapeDtypeStruct(x.shape, F32),
                   jax.ShapeDtypeStruct((bsz, 2, 2 * d_ff), F32)],
        scratch_shapes=[
            pltpu.VMEM((2, 2 * d_ff), F32),
            pltpu.VMEM((2, 2 * FFN_COLS // LANES, SUBLANES + sub, LANES), F32),
            pltpu.VMEM((2, FFN_COLS // LANES, sub, LANES), F32),
            pltpu.VMEM((subs, sub, d_model), F32),
            pltpu.VMEM((subs, sub, d_model), F32),
            pltpu.VMEM((subs, sub, d_model), BF16),
        ],
        compiler_params=pltpu.CompilerParams(
            dimension_semantics=("arbitrary", "arbitrary"), vmem_limit_bytes=VMEM_LIMIT),
        name="ffn_prompt",
    )(x, oa, ob, *fw)


def _ffn_sample(x, oa, ob, tap2_all, tap1_all, fw, layer, alpha, up_all):
    n, d_model = x.shape
    d_ff = fw[6].shape[1]
    whole = lambda a: pl.BlockSpec(a.shape, lambda i: (0,) * a.ndim, pipeline_mode=pl.Buffered(1))
    args = [x, oa, ob, tap2_all, tap1_all, *fw, up_all]
    in_specs = [whole(x), whole(oa), whole(ob), _resident(tap2_all, layer, 1), _resident(tap1_all, layer, 1)]
    in_specs += [_resident(a, layer, 1) for a in fw]
    in_specs.append(pl.BlockSpec(memory_space=pl.ANY))
    aliases = {len(args) - 1: 1}
    return pl.pallas_call(
        functools.partial(_ffn_sample_kernel, alpha=alpha, d_ff=d_ff),
        grid=(1,),
        in_specs=in_specs,
        out_specs=[pl.BlockSpec(x.shape, lambda i: (0, 0)),
                   pl.BlockSpec((None,) + up_all.shape[1:], lambda i: (layer, 0, 0))],
        out_shape=[jax.ShapeDtypeStruct(x.shape, F32), jax.ShapeDtypeStruct(up_all.shape, F32)],
        input_output_aliases=aliases,
        compiler_params=pltpu.CompilerParams(
            dimension_semantics=("arbitrary",), vmem_limit_bytes=VMEM_LIMIT),
        name="ffn_sample",
    )(*args)


def _s5_params(lam_re, lam_im, log_dt, b_re, b_im, c_re, c_im, d_skip, w_glu):
    groups, n_p = lam_re.shape
    lr = jnp.minimum(lam_re, -1e-4)
    li = lam_im
    dt = jnp.exp(log_dt)[:, None]
    mag = jnp.exp(lr * dt)
    ab_re = mag * jnp.cos(li * dt)
    ab_im = mag * jnp.sin(li * dt)
    den = lr * lr + li * li
    nr = ab_re - 1.0
    coef_re = (nr * lr + ab_im * li) / den
    coef_im = (ab_im * lr - nr * li) / den
    bb_re = coef_re[..., None] * b_re - coef_im[..., None] * b_im
    bb_im = coef_re[..., None] * b_im + coef_im[..., None] * b_re

    gpb = LANES // S5_GROUP_CH
    kblocks = groups // gpb
    eye = jnp.eye(gpb, dtype=F32)

    def b_blocks(bb):
        bb = bb.reshape(kblocks, gpb, n_p, S5_GROUP_CH)
        return jnp.einsum('mgpj,gh->mgjhp', bb, eye).reshape(kblocks, LANES, gpb * n_p).astype(BF16)

    gps = LANES // n_p
    slabs = groups // gps
    gpt = MXU_COLS // S5_GROUP_CH
    pos = (jnp.arange(slabs)[:, None] * gps + jnp.arange(gps)[None, :]) % gpt
    onehot = jax.nn.one_hot(pos, gpt, dtype=F32)

    def c_half(c):
        c = c.reshape(slabs, gps, S5_GROUP_CH, n_p)
        return jnp.einsum('sgjp,sgh->sgphj', c, onehot).reshape(slabs, gps * n_p, gpt * S5_GROUP_CH)

    cblk = jnp.concatenate([c_half(c_re), -c_half(c_im)], axis=1).astype(BF16)
    return (b_blocks(bb_re), b_blocks(bb_im), cblk,
            ab_re.reshape(1, groups * n_p), ab_im.reshape(1, groups * n_p),
            d_skip.reshape(1, -1), w_glu.astype(BF16))


def kernel(x_prompt, x_sample, state_hgrn, state_s5_re, state_s5_im, cache_ffn_conv, w_in, hgrn_lb_logits, hgrn_norm_g, s5_lambda_re, s5_lambda_im, s5_log_dt, s5_b_re, s5_b_im, s5_c_re, s5_c_im, s5_d, w_glu, w_out, ln1_g, ln1_b, w_ffn_up, ffn_conv_w, ffn_conv_b, w_ffn_down, ln2_g, ln2_b):
    depth = w_in.shape[0]
    d_h = hgrn_lb_logits.shape[-1]
    heads = d_h // HEAD_DIM
    n_dec = x_sample.shape[0]
    groups, n_p = s5_lambda_re.shape[1:]
    alpha = (2 * depth) ** 0.25

    sm = jax.nn.softmax(hgrn_lb_logits.astype(F32), axis=0)
    lower_bounds = jnp.cumsum(sm, axis=0) - sm[0:1]

    w_in_b = w_in.astype(BF16)
    s5p = jax.vmap(_s5_params)(s5_lambda_re, s5_lambda_im, s5_log_dt, s5_b_re, s5_b_im,
                               s5_c_re, s5_c_im, s5_d, w_glu)
    rows = lambda a: a.reshape(depth, 1, -1)
    fw = (w_out.astype(BF16), rows(ln1_g), rows(ln1_b), w_ffn_up.astype(BF16),
          ffn_conv_w, rows(ffn_conv_b), w_ffn_down.astype(BF16), rows(ln2_g), rows(ln2_b))
    h_re = state_s5_re.reshape(depth, n_dec, groups * n_p)
    h_im = state_s5_im.reshape(depth, n_dec, groups * n_p)
    tap2_all = cache_ffn_conv[:, :, 0, :]
    tap1_all = cache_ffn_conv[:, :, 1, :]

    xp = x_prompt
    xs = x_sample.reshape(n_dec, -1)
    outs = {k: [] for k in ("hp", "rp", "ip", "cp", "rs", "is")}
    hs_new = jnp.zeros(state_hgrn.shape, F32)
    up_new = jnp.zeros(tap1_all.shape, F32)
    for l in range(depth):
        lb = lower_bounds[l].reshape(1, d_h)
        ng = jnp.tile(hgrn_norm_g[l], heads).reshape(1, d_h)

        oa, s_new = _hgrn_prompt(xp, w_in_b, l, lb, ng)
        ob, re_new, im_new = _s5_prompt(xp, w_in_b, l, s5p)
        xp, cache_new = _ffn_prompt(xp, oa, ob, fw, l, alpha)
        outs["hp"].append(s_new)
        outs["rp"].append(re_new.reshape(-1, groups, n_p))
        outs["ip"].append(im_new.reshape(-1, groups, n_p))
        outs["cp"].append(cache_new)

        oa, ob, hs_new, re_new, im_new = _mix_sample(xs, w_in_b, l, lb, ng, state_hgrn, h_re, h_im, s5p, hs_new)
        xs, up_new = _ffn_sample(xs, oa, ob, tap2_all, tap1_all, fw, l, alpha, up_new)
        outs["rs"].append(re_new.reshape(n_dec, groups, n_p))
        outs["is"].append(im_new.reshape(n_dec, groups, n_p))

    st = {k: jnp.stack(v) for k, v in outs.items()}
    cs_new = jnp.stack([tap1_all, up_new], axis=2)
    return (xp, xs.reshape(x_sample.shape), st["hp"], st["rp"], st["ip"], st["cp"],
            hs_new, st["rs"], st["is"], cs_new)
```
